```python
import jax, jax.numpy as jnp
from jax import lax
import numpy as np

D_MODEL = 1024
BATCH = 8
SEQ = 4096
DEPTH = 4

CTX_LEN = 256
GRID_W = 64
N_MOD = 9
D_FF = 2816
NORM_EPS = 1e-6
RWKV_HEAD = 64
D_RWKV = 3 * D_MODEL // 4
RWKV_HEADS = D_RWKV // RWKV_HEAD
DECAY_RANK = 64
ICLR_RANK = 64
GATE_RANK = 128
DECAY_SCALE = 0.606531
GN_EPS = 64e-5
D_FOURIER = D_MODEL // 4
FOURIER_GROUPS = 4
FOURIER_GROUP = D_FOURIER // FOURIER_GROUPS
D_A_IN = 3 * D_RWKV + DECAY_RANK + ICLR_RANK + GATE_RANK
D_EVEN_IN = D_A_IN + D_FOURIER
D_MIX_EVEN = D_RWKV + D_FOURIER
D_CONV = 3 * D_MODEL // 4
CONV_WIDTH = 31
POOL_WIDTHS = (2, 4, 8, 16)
D_POOL = D_MODEL // 4
POOL_GROUP = D_POOL // len(POOL_WIDTHS)
D_ODD_IN = 2 * D_CONV + D_POOL
D_MIX_ODD = D_CONV + D_POOL

kernel_name = 'hybrid_rwkv7_fnet_conformer_pool_dit'


def rmsnorm(x, g):
    xf = x.astype(jnp.float32)
    y = xf * lax.rsqrt(jnp.mean(xf * xf, axis=-1, keepdims=True) + NORM_EPS)
    return y.astype(x.dtype) * g


def modulate(h, g, shift, scale):
    return rmsnorm(h, g) * (1 + scale) + shift


def swiglu(y, w_gu, w_down):
    gate, up = jnp.split(y @ w_gu, 2, axis=-1)
    return (jax.nn.silu(gate) * up) @ w_down


def half_ffn(h, m, off, g, w_gu, w_down):
    y = modulate(h, g, m[off], m[off + 1])
    return h + 0.5 * m[off + 2] * swiglu(y, w_gu, w_down)


def to_column_major(u):
    b, n = u.shape[0], u.shape[1]
    rows = n // GRID_W
    return u.reshape(b, rows, GRID_W, -1).transpose(0, 2, 1, 3).reshape(b, n, -1)


def from_column_major(u):
    b, n = u.shape[0], u.shape[1]
    rows = n // GRID_W
    return u.reshape(b, GRID_W, rows, -1).transpose(0, 2, 1, 3).reshape(b, n, -1)


def token_shift(z, mu):
    zp = jnp.pad(z, ((0, 0), (1, 1), (0, 0)))
    return z + mu * (0.5 * (zp[:, :-2] + zp[:, 2:]) - z)


def fourier_mix(u):
    b, n, _ = u.shape
    uf = u.astype(jnp.float32).reshape(b, n, FOURIER_GROUPS, FOURIER_GROUP)
    out = jnp.fft.fft2(uf, axes=(1, 3), norm='ortho').real
    return out.reshape(b, n, D_FOURIER).astype(u.dtype)


def rwkv_features(zA, w0, w_up, a0, a_up, g_up, k_k, k_a):
    b, n, _ = zA.shape
    heads = lambda u: u.reshape(b, n, RWKV_HEADS, RWKV_HEAD)
    s = [D_RWKV, 2 * D_RWKV, 3 * D_RWKV, 3 * D_RWKV + DECAY_RANK, 3 * D_RWKV + DECAY_RANK + ICLR_RANK]
    r, k, v, wd, ad, gd = jnp.split(zA, s, axis=-1)
    g = jax.nn.sigmoid(gd) @ g_up
    kk = heads(k * k_k).astype(jnp.float32)
    kk = kk / jnp.maximum(jnp.sqrt(jnp.sum(kk * kk, axis=-1, keepdims=True)), 1e-12)
    dirs = []
    for d in range(2):
        w = jnp.exp(-DECAY_SCALE * jax.nn.sigmoid(w0[d] + jnp.tanh(wd) @ w_up[d]))
        a = jax.nn.sigmoid(a0[d] + ad @ a_up[d])
        kd = k * (1 + (a - 1) * k_a)
        dirs.append((heads(w), heads(kd), kk * heads(a).astype(jnp.float32)))
    return heads(r), heads(v), kk, g, dirs


def wkv_scan(S0, r, w, k, v, kk, bb, reverse):
    tm = lambda u: jnp.moveaxis(u.astype(jnp.float32), 1, 0)
    xs = (tm(r), tm(w), tm(k), tm(v), tm(kk), tm(bb))

    def step(S, inp):
        rt, wt, kt, vt, kkt, bt = inp
        sk = jnp.einsum('bhij,bhj->bhi', S, kkt)
        S = S * wt[:, :, None, :] - sk[..., None] * bt[:, :, None, :] + vt[..., None] * kt[:, :, None, :]
        return S, jnp.einsum('bhij,bhj->bhi', S, rt)

    S, ys = lax.scan(step, S0, xs, reverse=reverse)
    return S, jnp.moveaxis(ys, 0, 1)


def rwkv_output(y, r, kds, v, g, r_k, gn_w, gn_b):
    b, n = y.shape[0], y.shape[1]
    mu = jnp.mean(y, axis=-1, keepdims=True)
    var = jnp.mean(jnp.square(y - mu), axis=-1, keepdims=True)
    yn = ((y - mu) * lax.rsqrt(var + GN_EPS)).reshape(b, n, D_RWKV) * gn_w + gn_b
    rk = r_k.reshape(RWKV_HEADS, RWKV_HEAD)
    coef = jnp.sum(r * kds[0] * rk, axis=-1, keepdims=True) + jnp.sum(r * kds[1] * rk, axis=-1, keepdims=True)
    bonus = (coef * v).reshape(b, n, D_RWKV)
    return (yn.astype(g.dtype) + bonus) * g


def even_mixer(y_ctx, y_lat, col_major, w_in, mu, w0, w_up, a0, a_up, g_up, k_k, k_a, r_k, gn_w, gn_b, w_out):
    z_ctx = y_ctx @ w_in
    z_lat = y_lat @ w_in
    f_ctx = fourier_mix(z_ctx[..., D_A_IN:])
    f_lat = fourier_mix(z_lat[..., D_A_IN:])
    zA_ctx = z_ctx[..., :D_A_IN]
    zA_lat = z_lat[..., :D_A_IN]
    if col_major:
        zA_lat = to_column_major(zA_lat)
    zA_ctx = token_shift(zA_ctx, mu)
    zA_lat = token_shift(zA_lat, mu)
    fargs = (w0, w_up, a0, a_up, g_up, k_k, k_a)
    r_c, v_c, kk_c, g_c, dirs_c = rwkv_features(zA_ctx, *fargs)
    r_l, v_l, kk_l, g_l, dirs_l = rwkv_features(zA_lat, *fargs)
    S0 = jnp.zeros((y_ctx.shape[0], RWKV_HEADS, RWKV_HEAD, RWKV_HEAD), jnp.float32)
    ys_c, ys_l = [], []
    for d in range(2):
        w_c, kd_c, b_c = dirs_c[d]
        w_l, kd_l, b_l = dirs_l[d]
        S_c, yc = wkv_scan(S0, r_c, w_c, kd_c, v_c, kk_c, b_c, reverse=(d == 1))
        _, yl = wkv_scan(S_c, r_l, w_l, kd_l, v_l, kk_l, b_l, reverse=(d == 1))
        ys_c.append(yc)
        ys_l.append(yl)
    o_ctx = rwkv_output(ys_c[0] + ys_c[1], r_c, (dirs_c[0][1], dirs_c[1][1]), v_c, g_c, r_k, gn_w, gn_b)
    o_lat = rwkv_output(ys_l[0] + ys_l[1], r_l, (dirs_l[0][1], dirs_l[1][1]), v_l, g_l, r_k, gn_w, gn_b)
    if col_major:
        o_lat = from_column_major(o_lat)
    out_ctx = jnp.concatenate([o_ctx, f_ctx], axis=-1) @ w_out
    out_lat = jnp.concatenate([o_lat, f_lat], axis=-1) @ w_out
    return out_ctx, out_lat


def centered_pool_minus_self(u, width):
    n = u.shape[1]
    uf = u.astype(jnp.float32)
    cs = jnp.pad(jnp.cumsum(uf, axis=1), ((0, 0), (1, 0), (0, 0)))
    t = jnp.arange(n)
    lo = jnp.maximum(t - width // 2, 0)
    hi = jnp.minimum(t + (width - 1 - width // 2), n - 1)
    total = jnp.take(cs, hi + 1, axis=1) - jnp.take(cs, lo, axis=1)
    mean = total / (hi - lo + 1).astype(jnp.float32)[None, :, None]
    return (mean - uf).astype(u.dtype)


def odd_mixer(y, w_in, conv_w, conv_b, cnorm_g, pool_w, pool_scale, w_out):
    b, n, _ = y.shape
    z = y @ w_in
    ga, gb, q = jnp.split(z, [D_CONV, 2 * D_CONV], axis=-1)
    u = ga * jax.nn.sigmoid(gb)
    u = lax.conv_general_dilated(u, conv_w[:, None, :], window_strides=(1,),
                                 padding=[(CONV_WIDTH // 2, CONV_WIDTH // 2)],
                                 dimension_numbers=('NWC', 'WIO', 'NWC'),
                                 feature_group_count=D_CONV) + conv_b
    u = jax.nn.silu(rmsnorm(u, cnorm_g))
    qg = jnp.split(q, len(POOL_WIDTHS), axis=-1)
    p = jnp.stack([centered_pool_minus_self(qi, wd) for qi, wd in zip(qg, POOL_WIDTHS)], axis=2)
    p = jnp.einsum('bngc,gcd->bngd', p, pool_w).reshape(b, n, D_POOL) * pool_scale
    return jnp.concatenate([u, p], axis=-1) @ w_out


def setup_inputs(seed: int = 0) -> dict:
    key = jax.random.key(seed)
    ks = iter(jax.random.split(key, 40))
    nrm = lambda shape, scale: scale * jax.random.normal(next(ks), shape, jnp.float32)
    ne, no = (DEPTH + 1) // 2, DEPTH // 2
    D = D_MODEL
    return {
        'x': nrm((BATCH, SEQ, D), 1.0),
        'c': nrm((BATCH, D), 1.0),
        'ctx': nrm((BATCH, CTX_LEN, D), 1.0),
        'c_ctx': nrm((D,), 1.0),
        'ada_w': nrm((DEPTH, D, N_MOD * D), 0.5 * D ** -0.5),
        'ada_b': nrm((DEPTH, N_MOD * D), 0.02),
        'norm_g': 1.0 + nrm((DEPTH, 3, D), 0.05),
        'ffn1_w_gu': nrm((DEPTH, D, 2 * D_FF), D ** -0.5),
        'ffn1_w_down': nrm((DEPTH, D_FF, D), D_FF ** -0.5),
        'ffn2_w_gu': nrm((DEPTH, D, 2 * D_FF), D ** -0.5),
        'ffn2_w_down': nrm((DEPTH, D_FF, D), D_FF ** -0.5),
        'e_w_in': nrm((ne, D, D_EVEN_IN), D ** -0.5),
        'e_mu': jax.random.uniform(next(ks), (ne, D_A_IN), jnp.float32),
        'e_w0': -1.0 + nrm((ne, 2, D_RWKV), 0.5),
        'e_w_up': nrm((ne, 2, DECAY_RANK, D_RWKV), DECAY_RANK ** -0.5),
        'e_a0': nrm((ne, 2, D_RWKV), 0.5),
        'e_a_up': nrm((ne, 2, ICLR_RANK, D_RWKV), ICLR_RANK ** -0.5),
        'e_g_up': nrm((ne, GATE_RANK, D_RWKV), GATE_RANK ** -0.5),
        'e_k_k': 1.0 + nrm((ne, D_RWKV), 0.1),
        'e_k_a': 1.0 + nrm((ne, D_RWKV), 0.1),
        'e_r_k': nrm((ne, D_RWKV), 0.1),
        'e_gn_w': 1.0 + nrm((ne, D_RWKV), 0.05),
        'e_gn_b': nrm((ne, D_RWKV), 0.02),
        'e_w_out': nrm((ne, D_MIX_EVEN, D), D_MIX_EVEN ** -0.5),
        'o_w_in': nrm((no, D, D_ODD_IN), D ** -0.5),
        'o_conv_w': nrm((no, CONV_WIDTH, D_CONV), CONV_WIDTH ** -0.5),
        'o_conv_b': nrm((no, D_CONV), 0.02),
        'o_cnorm_g': 1.0 + nrm((no, D_CONV), 0.05),
        'o_pool_w': nrm((no, len(POOL_WIDTHS), POOL_GROUP, POOL_GROUP), POOL_GROUP ** -0.5),
        'o_pool_scale': 1.0 + nrm((no, D_POOL), 0.1),
        'o_w_out': nrm((no, D_MIX_ODD, D), D_MIX_ODD ** -0.5),
        'final_g': 1.0 + nrm((D,), 0.05),
    }


def reference(x, c, ctx, c_ctx, ada_w, ada_b, norm_g, ffn1_w_gu, ffn1_w_down, ffn2_w_gu, ffn2_w_down,
              e_w_in, e_mu, e_w0, e_w_up, e_a0, e_a_up, e_g_up, e_k_k, e_k_a, e_r_k, e_gn_w, e_gn_b, e_w_out,
              o_w_in, o_conv_w, o_conv_b, o_cnorm_g, o_pool_w, o_pool_scale, o_w_out, final_g):
    h_lat, h_ctx = x, ctx
    for i in range(DEPTH):
        j = i // 2
        with_ctx = not (i == DEPTH - 1 and i % 2 == 1)
        m_lat = jnp.split((jax.nn.silu(c) @ ada_w[i] + ada_b[i])[:, None, :], N_MOD, axis=-1)
        if with_ctx:
            m_ctx = jnp.split((jax.nn.silu(c_ctx) @ ada_w[i] + ada_b[i])[None, None, :], N_MOD, axis=-1)
            h_ctx = half_ffn(h_ctx, m_ctx, 0, norm_g[i, 0], ffn1_w_gu[i], ffn1_w_down[i])
        h_lat = half_ffn(h_lat, m_lat, 0, norm_g[i, 0], ffn1_w_gu[i], ffn1_w_down[i])
        y_lat = modulate(h_lat, norm_g[i, 1], m_lat[3], m_lat[4])
        if i % 2 == 0:
            y_ctx = modulate(h_ctx, norm_g[i, 1], m_ctx[3], m_ctx[4])
            o_ctx, o_lat = even_mixer(y_ctx, y_lat, j % 2 == 1, e_w_in[j], e_mu[j], e_w0[j], e_w_up[j],
                                      e_a0[j], e_a_up[j], e_g_up[j], e_k_k[j], e_k_a[j], e_r_k[j],
                                      e_gn_w[j], e_gn_b[j], e_w_out[j])
            h_ctx = h_ctx + m_ctx[5] * o_ctx
            h_lat = h_lat + m_lat[5] * o_lat
        else:
            odd_args = (o_w_in[j], o_conv_w[j], o_conv_b[j], o_cnorm_g[j], o_pool_w[j], o_pool_scale[j], o_w_out[j])
            h_lat = h_lat + m_lat[5] * odd_mixer(y_lat, *odd_args)
            if with_ctx:
                y_ctx = modulate(h_ctx, norm_g[i, 1], m_ctx[3], m_ctx[4])
                h_ctx = h_ctx + m_ctx[5] * odd_mixer(y_ctx, *odd_args)
        if with_ctx:
            h_ctx = half_ffn(h_ctx, m_ctx, 6, norm_g[i, 2], ffn2_w_gu[i], ffn2_w_down[i])
        h_lat = half_ffn(h_lat, m_lat, 6, norm_g[i, 2], ffn2_w_gu[i], ffn2_w_down[i])
    return rmsnorm(h_lat, final_g)
```

```python
import functools

import jax
import jax.numpy as jnp
import numpy as np
from jax import lax
from jax.experimental import pallas as pl
from jax.experimental.pallas import tpu as pltpu

F32 = jnp.float32
BF16 = jnp.bfloat16

N_MOD = 9
NORM_EPS = 1e-6
GRID_W = 64
HEAD = 64
PAIR = 2 * HEAD
DECAY_RANK = 64
ICLR_RANK = 64
GATE_RANK = 128
DECAY_SCALE = 0.606531
GN_EPS = 64e-5
FOURIER_GROUP = 64
CONV_WIDTH = 31
POOL_WIDTHS = (2, 4, 8, 16)
CHUNK = 64
HALO = 16
MOD_ROWS = 16
VMEM_LIMIT = 56 * 1024 * 1024


def _cparams(sem):
    return pltpu.CompilerParams(dimension_semantics=sem, vmem_limit_bytes=VMEM_LIMIT)


def _dot(a, b):
    return jnp.dot(a, b, preferred_element_type=F32)


def _dot_nt(a, b):
    return lax.dot_general(a, b, (((1,), (1,)), ((), ())), preferred_element_type=F32)


def _split2(x):
    hi = x.astype(BF16)
    lo = (x - hi.astype(F32)).astype(BF16)
    return hi, lo


def _dot_x2(x, w_bf16):
    hi, lo = _split2(x)
    return _dot(hi, w_bf16) + _dot(lo, w_bf16)


def _dot_x3(x, w):
    xh, xl = _split2(x)
    wh, wl = _split2(w)
    return _dot(xh, wh) + (_dot(xh, wl) + _dot(xl, wh))


def _sigmoid(x):
    return 1.0 / (1.0 + jnp.exp(-x))


def _silu(x):
    return x * _sigmoid(x)


def _modulate(h, g, shift, scale):
    y = h * lax.rsqrt(jnp.mean(h * h, axis=-1, keepdims=True) + NORM_EPS)
    return y * g * (1.0 + scale) + shift


def _ada_kernel(c_ref, w_ref, b_ref, o_ref):
    o_ref[0] = _dot_x3(_silu(c_ref[...]), w_ref[0]) + b_ref[0]


def _ada_table(cc, ada_w, ada_b):
    depth, d, nd = ada_w.shape
    tn = 1024
    return pl.pallas_call(
        _ada_kernel,
        grid=(depth, nd // tn),
        in_specs=[pl.BlockSpec((MOD_ROWS, d), lambda i, j: (0, 0)),
                  pl.BlockSpec((1, d, tn), lambda i, j: (i, 0, j)),
                  pl.BlockSpec((1, 1, tn), lambda i, j: (i, 0, j))],
        out_specs=pl.BlockSpec((1, MOD_ROWS, tn), lambda i, j: (i, 0, j)),
        out_shape=jax.ShapeDtypeStruct((depth, MOD_ROWS, nd), F32),
        compiler_params=_cparams(("parallel", "parallel")),
        name="ada_table",
    )(cc, ada_w, ada_b.reshape(depth, 1, nd))


class _Tokens:
    def __init__(self, batch, n_lat, n_ctx, d):
        self.batch, self.n_lat, self.n_ctx, self.d = batch, n_lat, n_ctx, d
        self.t_lat = batch * n_lat
        self.t_all = batch * (n_lat + n_ctx)

    def rows(self, with_ctx):
        return self.t_all if with_ctx else self.t_lat

    def mod_index(self, tm):
        lat_tiles, per_batch, ctx_row = self.t_lat // tm, self.n_lat // tm, self.batch
        return lambda t, *_: (jnp.where(t < lat_tiles, t // per_batch, ctx_row), 0, 0)


def _mod_slices(m_ref, off, d):
    return [m_ref[0, :, (off + k) * d:(off + k + 1) * d] for k in range(3)]


def _ffn_kernel(h_ref, m_ref, g_ref, wg_ref, wu_ref, wd_ref, fg_ref, o_ref, y_scr, acc_scr, *, off, final):
    f, nf, d = pl.program_id(1), pl.num_programs(1), h_ref.shape[1]
    shift, scale, gate = _mod_slices(m_ref, off, d)

    @pl.when(f == 0)
    def _():
        y_scr[...] = _modulate(h_ref[...], g_ref[...], shift, scale).astype(BF16)
        acc_scr[...] = jnp.zeros_like(acc_scr)

    y = y_scr[...]
    act = (_silu(_dot(y, wg_ref[...])) * _dot(y, wu_ref[...])).astype(BF16)
    acc_scr[...] += _dot(act, wd_ref[...])

    @pl.when(f == nf - 1)
    def _():
        out = h_ref[...] + (0.5 * gate) * acc_scr[...]
        if final:
            out = out * lax.rsqrt(jnp.mean(out * out, axis=-1, keepdims=True) + NORM_EPS) * fg_ref[...]
        o_ref[...] = out


def _half_ffn(tok, h, mods, g, w_gu, w_down, final_g, *, off, with_ctx, final=False):
    d, ff = tok.d, w_down.shape[0]
    tm, tf = 512, ff // 2
    nf = ff // tf
    rows = tok.rows(with_ctx)
    return pl.pallas_call(
        functools.partial(_ffn_kernel, off=off, final=final),
        grid=(rows // tm, nf),
        in_specs=[pl.BlockSpec((tm, d), lambda t, f: (t, 0)),
                  pl.BlockSpec((1, 1, N_MOD * d), tok.mod_index(tm)),
                  pl.BlockSpec((1, d), lambda t, f: (0, 0)),
                  pl.BlockSpec((d, tf), lambda t, f: (0, f)),
                  pl.BlockSpec((d, tf), lambda t, f: (0, nf + f)),
                  pl.BlockSpec((tf, d), lambda t, f: (f, 0)),
                  pl.BlockSpec((1, d), lambda t, f: (0, 0))],
        out_specs=pl.BlockSpec((tm, d), lambda t, f: (t, 0)),
        out_shape=jax.ShapeDtypeStruct((rows, d), F32),
        scratch_shapes=[pltpu.VMEM((tm, d), BF16), pltpu.VMEM((tm, d), F32)],
        compiler_params=_cparams(("parallel", "arbitrary")),
        name="half_ffn",
    )(h, mods, g.reshape(1, d), w_gu, w_gu, w_down, final_g.reshape(1, d))


def _mix_in_kernel(h_ref, m_ref, g_ref, w_ref, z_ref, y_scr):
    d = h_ref.shape[1]
    shift, scale, _ = _mod_slices(m_ref, 3, d)

    @pl.when(pl.program_id(1) == 0)
    def _():
        y_scr[...] = _modulate(h_ref[...], g_ref[...], shift, scale).astype(BF16)

    z_ref[...] = _dot(y_scr[...], w_ref[...])


def _mix_in(tok, h, mods, g, w_in, *, with_ctx):
    d, d_in = w_in.shape
    tm, tn = 512, d_in // 2
    rows = tok.rows(with_ctx)
    return pl.pallas_call(
        _mix_in_kernel,
        grid=(rows // tm, d_in // tn),
        in_specs=[pl.BlockSpec((tm, d), lambda t, j: (t, 0)),
                  pl.BlockSpec((1, 1, N_MOD * d), tok.mod_index(tm)),
                  pl.BlockSpec((1, d), lambda t, j: (0, 0)),
                  pl.BlockSpec((d, tn), lambda t, j: (0, j))],
        out_specs=pl.BlockSpec((tm, tn), lambda t, j: (t, j)),
        out_shape=jax.ShapeDtypeStruct((rows, d_in), F32),
        scratch_shapes=[pltpu.VMEM((tm, d), BF16)],
        compiler_params=_cparams(("parallel", "arbitrary")),
        name="mix_in",
    )(h, mods, g.reshape(1, d), w_in)


def _mix_out_kernel(h_ref, m_ref, x_ref, w_ref, o_ref):
    d = h_ref.shape[1]
    gate = m_ref[0, :, 5 * d:6 * d]
    o_ref[...] = h_ref[...] + gate * _dot(x_ref[...].astype(BF16), w_ref[...])


def _mix_out(tok, h, mods, x, w_out):
    d = tok.d
    tm = 512
    rows = x.shape[0]
    return pl.pallas_call(
        _mix_out_kernel,
        grid=(rows // tm,),
        in_specs=[pl.BlockSpec((tm, d), lambda t: (t, 0)),
                  pl.BlockSpec((1, 1, N_MOD * d), tok.mod_index(tm)),
                  pl.BlockSpec((tm, w_out.shape[0]), lambda t: (t, 0)),
                  pl.BlockSpec(w_out.shape, lambda t: (0, 0))],
        out_specs=pl.BlockSpec((tm, d), lambda t: (t, 0)),
        out_shape=jax.ShapeDtypeStruct((rows, d), F32),
        compiler_params=_cparams(("parallel",)),
        name="mix_out",
    )(h, mods, x, w_out)


def _feat_kernel(z_ref, zp_ref, zn_ref, mu_ref, wup_ref, aup_ref, gup_ref, w0_ref, a0_ref, kk_ref, ka_ref,
                 rk_ref, ones_ref, r_out, v_out, kap_out, lw_out, kd_out, b_out, g_out, bonus_out,
                 *, ctx_tiles, d_rwkv):
    i, last = pl.program_id(1), pl.num_programs(1) - 1
    tt = z_ref.shape[1]
    z = z_ref[0]
    first_of_seq = jnp.logical_or(i == 0, i == ctx_tiles)
    last_of_seq = jnp.logical_or(i == ctx_tiles - 1, i == last)
    prev_row = jnp.where(first_of_seq, 0.0, zp_ref[0, 7:8, :])
    next_row = jnp.where(last_of_seq, 0.0, zn_ref[0, 0:1, :])
    row = lax.broadcasted_iota(jnp.int32, (tt, 1), 0)
    z_prev = jnp.where(row == 0, prev_row, pltpu.roll(z, 1, axis=0))
    z_next = jnp.where(row == tt - 1, next_row, pltpu.roll(z, tt - 1, axis=0))
    z = z + mu_ref[...] * (0.5 * (z_prev + z_next) - z)

    dr = d_rwkv
    r, k, v = z[:, :dr], z[:, dr:2 * dr], z[:, 2 * dr:3 * dr]
    low = z[:, 3 * dr:3 * dr + DECAY_RANK + ICLR_RANK]
    gd = z[:, 3 * dr + DECAY_RANK + ICLR_RANK:]
    ones = ones_ref[...]

    kap = k * kk_ref[...]
    norm = jnp.sqrt(_dot_x2(kap * kap, ones))
    kap = kap / jnp.maximum(norm, 1e-12)
    tanh_low = jnp.tanh(low)
    kd_sum = jnp.zeros_like(k)
    for dd in range(2):
        lw_out[dd, 0] = -DECAY_SCALE * _sigmoid(w0_ref[dd] + _dot_x3(tanh_low, wup_ref[dd]))
        a = _sigmoid(a0_ref[dd] + _dot_x3(low, aup_ref[dd]))
        kd = k * (1.0 + (a - 1.0) * ka_ref[...])
        kd_out[dd, 0] = kd
        b_out[dd, 0] = kap * a
        kd_sum = kd_sum + kd
    r_out[0] = r
    v_out[0] = v
    kap_out[0] = kap
    g_out[0] = _dot_x3(_sigmoid(gd), gup_ref[...])
    bonus_out[0] = _dot_x2(r * kd_sum * rk_ref[...], ones) * v


def _rwkv_features(z_seq, n_ctx, p):
    batch, t, da = z_seq.shape
    dr = p["k_k"].shape[-1]
    tt = 256
    ctx_tiles = n_ctx // tt
    hb = tt // 8
    n_hblk = t // 8
    row = lambda name: p[name].reshape(1, dr)
    seq_spec = pl.BlockSpec((1, tt, dr), lambda b, i: (b, i, 0))
    dir_spec = pl.BlockSpec((2, 1, tt, dr), lambda b, i: (0, b, i, 0))
    full = lambda a: pl.BlockSpec(a.shape, lambda b, i: (0,) * a.ndim)
    seq_shape = jax.ShapeDtypeStruct((batch, t, dr), F32)
    dir_shape = jax.ShapeDtypeStruct((2, batch, t, dr), F32)
    consts = [p["mu"].reshape(1, da), p["w_up_pad"], p["a_up_pad"], p["g_up"],
              p["w0"].reshape(2, 1, dr), p["a0"].reshape(2, 1, dr), row("k_k"), row("k_a"), row("r_k"),
              p["head_ones"]]
    return pl.pallas_call(
        functools.partial(_feat_kernel, ctx_tiles=ctx_tiles, d_rwkv=dr),
        grid=(batch, t // tt),
        in_specs=[pl.BlockSpec((1, tt, da), lambda b, i: (b, i, 0)),
                  pl.BlockSpec((1, 8, da), lambda b, i: (b, jnp.maximum(i * hb - 1, 0), 0)),
                  pl.BlockSpec((1, 8, da), lambda b, i: (b, jnp.minimum((i + 1) * hb, n_hblk - 1), 0))]
                 + [full(a) for a in consts],
        out_specs=[seq_spec, seq_spec, seq_spec, dir_spec, dir_spec, dir_spec, seq_spec, seq_spec],
        out_shape=[seq_shape, seq_shape, seq_shape, dir_shape, dir_shape, dir_shape, seq_shape, seq_shape],
        compiler_params=_cparams(("parallel", "parallel")),
        name="rwkv_features",
    )(z_seq, z_seq, z_seq, *consts)


def _scan_kernel(r_ref, v_ref, kap_ref, lw_ref, kd_ref, b_ref, y_ref, s_scr):
    c = pl.program_id(2)
    rev = pl.program_id(0) == 1
    ch = r_ref.shape[1]
    n_pairs = r_ref.shape[2] // PAIR

    @pl.when(c == 0)
    def _():
        s_scr[...] = jnp.zeros_like(s_scr)

    ti = lax.broadcasted_iota(jnp.int32, (ch, ch), 0)
    tj = lax.broadcasted_iota(jnp.int32, (ch, ch), 1)
    sign = jnp.where(rev, -1, 1)
    cum_mat = jnp.where(sign * (tj - ti) <= 0, 1.0, 0.0).astype(BF16)
    ri = lax.broadcasted_iota(jnp.int32, (PAIR, PAIR), 0)
    ci = lax.broadcasted_iota(jnp.int32, (PAIR, PAIR), 1)
    same_head = (ri // ch) == (ci // ch)
    order = sign * (ci % ch - ri % ch)
    strict = jnp.logical_and(same_head, order < 0)
    incl = jnp.logical_and(same_head, order <= 0)
    lane_head = lax.broadcasted_iota(jnp.int32, (ch, PAIR), 1) // HEAD
    head0, head1 = lane_head == 0, lane_head == 1

    lw_all = lw_ref[0, 0]
    l1 = lw_all.astype(BF16)
    rem = lw_all - l1.astype(F32)
    l2 = rem.astype(BF16)
    l3 = (rem - l2.astype(F32)).astype(BF16)
    cl_all = _dot(cum_mat, l1) + (_dot(cum_mat, l2) + _dot(cum_mat, l3))

    def stack_heads(x):
        return jnp.concatenate([jnp.where(head0, x, 0.0), jnp.where(head1, x, 0.0)], axis=0)

    for p in range(n_pairs):
        sl = slice(p * PAIR, (p + 1) * PAIR)
        lw, cl = lw_all[:, sl], cl_all[:, sl]
        r, v, kap, kd, b = r_ref[0, :, sl], v_ref[0, :, sl], kap_ref[0, :, sl], kd_ref[0, 0, :, sl], b_ref[0, 0, :, sl]
        tot = jnp.sum(lw, axis=0, keepdims=True)
        g_inv = jnp.exp(-cl)
        kq = kap * jnp.exp(cl - lw)
        rq = r * jnp.exp(cl)
        kdd, bdd = kd * g_inv, b * g_inv
        g_rem = jnp.exp(tot - cl)
        s0 = s_scr[p]

        m4 = jnp.concatenate([stack_heads(kq), stack_heads(rq)], axis=0).astype(BF16)
        n4 = jnp.concatenate([bdd, bdd, kdd, kdd], axis=0).astype(BF16)
        gram = _dot_nt(m4, n4)
        l_b = jnp.where(strict, gram[:PAIR, :PAIR], 0.0)
        l_k = jnp.where(strict, gram[:PAIR, PAIR:], 0.0)
        p_b = jnp.where(incl, gram[PAIR:, :PAIR], 0.0)
        p_k = jnp.where(incl, gram[PAIR:, PAIR:], 0.0)
        from_state = _dot_nt(m4, s0.astype(BF16))

        v2 = jnp.concatenate([v, v], axis=0).astype(BF16)
        u = from_state[:PAIR] + jnp.where(same_head, _dot(l_k.astype(BF16), v2), 0.0)
        lp = l_b
        u = u - _dot(lp.astype(BF16), u.astype(BF16))
        for _ in range(5):
            lp = _dot(lp.astype(BF16), lp.astype(BF16))
            u = u + _dot(lp.astype(BF16), u.astype(BF16))

        pkv_pbu = _dot(jnp.concatenate([p_k, -p_b], axis=1).astype(BF16),
                       jnp.concatenate([v2, u.astype(BF16)], axis=0))
        y2 = from_state[PAIR:] + jnp.where(same_head, pkv_pbu, 0.0)
        y_ref[0, 0, :, sl] = y2[:ch] + y2[ch:]

        u_pair = u[:ch] + u[ch:]
        vu_t = jnp.concatenate([v, -u_pair], axis=0).T.astype(BF16)
        kb = jnp.concatenate([kd * g_rem, b * g_rem], axis=0).astype(BF16)
        s_scr[p] = s0 * jnp.exp(tot) + jnp.where(same_head, _dot(vu_t, kb), 0.0)


def _rwkv_scan(r, v, kap, lw, kd, b, n_ctx):
    batch, t, dr = r.shape
    n_chunks, ctx_chunks = t // CHUNK, n_ctx // CHUNK

    def chunk_of(d, c):
        rev = jnp.where(c < ctx_chunks, ctx_chunks - 1 - c, n_chunks + ctx_chunks - 1 - c)
        return jnp.where(d == 0, c, rev)

    seq_spec = pl.BlockSpec((1, CHUNK, dr), lambda d, bb, c: (bb, chunk_of(d, c), 0))
    dir_spec = pl.BlockSpec((1, 1, CHUNK, dr), lambda d, bb, c: (d, bb, chunk_of(d, c), 0))
    return pl.pallas_call(
        _scan_kernel,
        grid=(2, batch, n_chunks),
        in_specs=[seq_spec, seq_spec, seq_spec, dir_spec, dir_spec, dir_spec],
        out_specs=dir_spec,
        out_shape=jax.ShapeDtypeStruct((2, batch, t, dr), F32),
        scratch_shapes=[pltpu.VMEM((dr // PAIR, PAIR, PAIR), F32)],
        compiler_params=_cparams(("parallel", "parallel", "arbitrary")),
        name="rwkv_scan",
    )(r, v, kap, lw, kd, b)


def _rwkv_out_kernel(y_ref, g_ref, bonus_ref, gnw_ref, gnb_ref, mean_ref, o_ref):
    y = y_ref[0, 0] + y_ref[1, 0]
    mean_mat = mean_ref[...]
    dev = y - _dot_x2(y, mean_mat)
    var = _dot_x2(dev * dev, mean_mat)
    yn = dev * lax.rsqrt(var + GN_EPS) * gnw_ref[...] + gnb_ref[...]
    o_ref[0] = (yn + bonus_ref[0]) * g_ref[0]


def _rwkv_out(y, g, bonus, p):
    batch, t, dr = g.shape
    tt = 256
    assert t % tt == 0
    seq_spec = pl.BlockSpec((1, tt, dr), lambda b, i: (b, i, 0))
    vec_spec = pl.BlockSpec((1, dr), lambda b, i: (0, 0))
    return pl.pallas_call(
        _rwkv_out_kernel,
        grid=(batch, t // tt),
        in_specs=[pl.BlockSpec((2, 1, tt, dr), lambda b, i: (0, b, i, 0)), seq_spec, seq_spec, vec_spec, vec_spec,
                  pl.BlockSpec((dr, dr), lambda b, i: (0, 0))],
        out_specs=seq_spec,
        out_shape=jax.ShapeDtypeStruct((batch, t, dr), F32),
        compiler_params=_cparams(("parallel", "parallel")),
        name="rwkv_out",
    )(y, g, bonus, p["gn_w"].reshape(1, dr), p["gn_b"].reshape(1, dr), p["head_mean"])


def _fourier_kernel(cs_ref, u_ref, cg_ref, sg_ref, o_ref, stacked_scr):
    n = u_ref.shape[1]

    @pl.when(pl.program_id(1) == 0)
    def _():
        u_hi, u_lo = _split2(u_ref[0])
        stacked_scr[:n] = (_dot(u_hi, cg_ref[...]) + _dot(u_lo, cg_ref[...])).astype(BF16)
        stacked_scr[n:] = (_dot(u_hi, sg_ref[...]) + _dot(u_lo, sg_ref[...])).astype(BF16)

    o_ref[0] = _dot(cs_ref[...], stacked_scr[...])


def _dft_tables(n, groups):
    k = lax.broadcasted_iota(jnp.int32, (n, n), 0)
    m = lax.broadcasted_iota(jnp.int32, (n, n), 1)
    ang = ((k * m) % n).astype(F32) * (2.0 * np.pi / n)
    scale = 1.0 / np.sqrt(n * FOURIER_GROUP)
    cs = jnp.concatenate([jnp.cos(ang) * scale, -jnp.sin(ang) * scale], axis=1).astype(BF16)
    kg = np.arange(FOURIER_GROUP)
    ang_g = 2.0 * np.pi * ((kg[:, None] * kg[None, :]) % FOURIER_GROUP) / FOURIER_GROUP
    eye = np.eye(groups)
    cg = jnp.asarray(np.kron(eye, np.cos(ang_g)), F32).astype(BF16)
    sg = jnp.asarray(np.kron(eye, np.sin(ang_g)), F32).astype(BF16)
    return cs, cg, sg


def _fourier_mix(u):
    batch, n, ch = u.shape
    cs, cg, sg = _dft_tables(n, ch // FOURIER_GROUP)
    tn = min(n, 512)
    return pl.pallas_call(
        _fourier_kernel,
        grid=(batch, n // tn),
        in_specs=[pl.BlockSpec((tn, 2 * n), lambda b, i: (i, 0)),
                  pl.BlockSpec((1, n, ch), lambda b, i: (b, 0, 0)),
                  pl.BlockSpec((ch, ch), lambda b, i: (0, 0)),
                  pl.BlockSpec((ch, ch), lambda b, i: (0, 0))],
        out_specs=pl.BlockSpec((1, tn, ch), lambda b, i: (b, i, 0)),
        out_shape=jax.ShapeDtypeStruct((batch, n, ch), F32),
        scratch_shapes=[pltpu.VMEM((2 * n, ch), BF16)],
        compiler_params=_cparams(("parallel", "arbitrary")),
        name="fourier_mix",
    )(cs, u, cg, sg)


def _odd_kernel(h_ref, m_ref, z_ref, zp_ref, zn_ref, cw_ref, cb_ref, cg_ref, pw_ref, ps_ref, wu_ref, wp_ref,
                o_ref, u_scr, *, lat_tiles, lat_seq_tiles, n_lat, n_ctx, d_conv):
    i = pl.program_id(0)
    tt, d = h_ref.shape
    is_lat = i < lat_tiles
    seq_tile = jnp.where(is_lat, i % lat_seq_tiles, 0)
    seq_tiles = jnp.where(is_lat, lat_seq_tiles, n_ctx // tt)
    seq_len = jnp.where(is_lat, n_lat, n_ctx)
    zp = jnp.where(seq_tile == 0, 0.0, zp_ref[...])
    zn = jnp.where(seq_tile == seq_tiles - 1, 0.0, zn_ref[...])
    ext = jnp.concatenate([zp, z_ref[...], zn], axis=0)

    dc = d_conv
    u_scr[...] = ext[:, :dc] * _sigmoid(ext[:, dc:2 * dc])
    acc = jnp.zeros((tt, dc), F32) + cb_ref[...]
    for k in range(CONV_WIDTH):
        acc = acc + cw_ref[k:k + 1, :] * u_scr[pl.ds(HALO - CONV_WIDTH // 2 + k, tt), :]
    u = acc * lax.rsqrt(jnp.mean(acc * acc, axis=-1, keepdims=True) + NORM_EPS) * cg_ref[...]
    u = _silu(u)

    q = ext[:, 2 * dc:]
    rows = tt + 2 * HALO
    sums, win, step = [], q, 1
    for width in POOL_WIDTHS:
        if width == 2:
            win = q + pltpu.roll(q, 1, axis=0)
        else:
            win = pltpu.roll(win, step, axis=0) + pltpu.roll(win, rows - step, axis=0)
            step *= 2
        sums.append(win[HALO:HALO + tt])
    n_groups = len(POOL_WIDTHS)
    group = lax.broadcasted_iota(jnp.int32, (tt, q.shape[1]), 1) // (q.shape[1] // n_groups)
    pos = seq_tile * tt + lax.broadcasted_iota(jnp.int32, (tt, q.shape[1]), 0)
    total = sums[-1]
    width_of = jnp.full(group.shape, POOL_WIDTHS[-1], jnp.int32)
    for gi in range(n_groups - 2, -1, -1):
        total = jnp.where(group == gi, sums[gi], total)
        width_of = jnp.where(group == gi, POOL_WIDTHS[gi], width_of)
    lo = jnp.maximum(pos - width_of // 2, 0)
    hi = jnp.minimum(pos + (width_of - 1 - width_of // 2), seq_len - 1)
    pooled = total / (hi - lo + 1).astype(F32) - q[HALO:HALO + tt]
    pooled = _dot(pooled.astype(BF16), pw_ref[...]) * ps_ref[...]

    mixed = _dot(u.astype(BF16), wu_ref[...]) + _dot(pooled.astype(BF16), wp_ref[...])
    o_ref[...] = h_ref[...] + m_ref[0, :, 5 * d:6 * d] * mixed


def _odd_mixer(tok, h, mods, z, p, *, with_ctx):
    d, d_in = tok.d, z.shape[1]
    dc = p["conv_w"].shape[1]
    dp = d_in - 2 * dc
    tt = 256
    rows = tok.rows(with_ctx)
    hb, n_hblk = tt // HALO, z.shape[0] // HALO
    full = lambda a: pl.BlockSpec(a.shape, lambda i: (0,) * a.ndim)
    consts = [p["conv_w"], p["conv_b"].reshape(1, dc), p["cnorm_g"].reshape(1, dc), p["pool_w_bd"],
              p["pool_scale"].reshape(1, dp), p["w_out_u"], p["w_out_p"]]
    return pl.pallas_call(
        functools.partial(_odd_kernel, lat_tiles=tok.t_lat // tt, lat_seq_tiles=tok.n_lat // tt,
                          n_lat=tok.n_lat, n_ctx=tok.n_ctx, d_conv=dc),
        grid=(rows // tt,),
        in_specs=[pl.BlockSpec((tt, d), lambda i: (i, 0)),
                  pl.BlockSpec((1, 1, N_MOD * d), tok.mod_index(tt)),
                  pl.BlockSpec((tt, d_in), lambda i: (i, 0)),
                  pl.BlockSpec((HALO, d_in), lambda i: (jnp.maximum(i * hb - 1, 0), 0)),
                  pl.BlockSpec((HALO, d_in), lambda i: (jnp.minimum((i + 1) * hb, n_hblk - 1), 0))]
                 + [full(a) for a in consts],
        out_specs=pl.BlockSpec((tt, d), lambda i: (i, 0)),
        out_shape=jax.ShapeDtypeStruct((rows, d), F32),
        scratch_shapes=[pltpu.VMEM((tt + 2 * HALO, dc), F32)],
        compiler_params=_cparams(("parallel",)),
        name="odd_mixer",
    )(h, mods, z, z, z, *consts)


def _to_column_major(u):
    b, n, c = u.shape
    return u.reshape(b, n // GRID_W, GRID_W, c).transpose(0, 2, 1, 3).reshape(b, n, c)


def _from_column_major(u):
    b, n, c = u.shape
    return u.reshape(b, GRID_W, n // GRID_W, c).transpose(0, 2, 1, 3).reshape(b, n, c)


def _even_mixer(tok, h, mods, norm_g, p, col_major):
    z = _mix_in(tok, h, mods, norm_g, p["w_in"], with_ctx=True)
    b, n, nc = tok.batch, tok.n_lat, tok.n_ctx
    da = p["mu"].shape[-1]
    z_lat = z[:tok.t_lat].reshape(b, n, -1)
    z_ctx = z[tok.t_lat:].reshape(b, nc, -1)
    f_lat = _fourier_mix(z_lat[..., da:])
    f_ctx = _fourier_mix(z_ctx[..., da:])
    za_lat = z_lat[..., :da]
    if col_major:
        za_lat = _to_column_major(za_lat)
    z_seq = jnp.concatenate([z_ctx[..., :da], za_lat], axis=1)
    r, v, kap, lw, kd, bb, g, bonus = _rwkv_features(z_seq, nc, p)
    y = _rwkv_scan(r, v, kap, lw, kd, bb, nc)
    o = _rwkv_out(y, g, bonus, p)
    o_ctx, o_lat = o[:, :nc], o[:, nc:]
    if col_major:
        o_lat = _from_column_major(o_lat)
    x_lat = jnp.concatenate([o_lat, f_lat], axis=-1).reshape(tok.t_lat, -1)
    x_ctx = jnp.concatenate([o_ctx, f_ctx], axis=-1).reshape(b * nc, -1)
    return _mix_out(tok, h, mods, jnp.concatenate([x_lat, x_ctx], axis=0), p["w_out"])


def _pad_rows(w, before, total):
    return jnp.pad(w, ((0, 0), (before, total - before - w.shape[1]), (0, 0)))


def _block_diag(blocks):
    g, a, b = blocks.shape
    eye = jnp.eye(g, dtype=blocks.dtype)
    return (eye[:, None, :, None] * blocks[:, :, None, :]).reshape(g * a, g * b)


def kernel(x, c, ctx, c_ctx, ada_w, ada_b, norm_g, ffn1_w_gu, ffn1_w_down, ffn2_w_gu, ffn2_w_down,
           e_w_in, e_mu, e_w0, e_w_up, e_a0, e_a_up, e_g_up, e_k_k, e_k_a, e_r_k, e_gn_w, e_gn_b, e_w_out,
           o_w_in, o_conv_w, o_conv_b, o_cnorm_g, o_pool_w, o_pool_scale, o_w_out, final_g):
    batch, n_lat, d = x.shape
    n_ctx = ctx.shape[1]
    depth = ada_w.shape[0]
    tok = _Tokens(batch, n_lat, n_ctx, d)
    assert batch < MOD_ROWS and n_lat % 512 == 0 and (batch * n_ctx) % 512 == 0 and n_ctx % 256 == 0

    cc = jnp.zeros((MOD_ROWS, d), F32).at[:batch].set(c).at[batch].set(c_ctx)
    mods_all = _ada_table(cc, ada_w, ada_b).reshape(depth, MOD_ROWS, 1, N_MOD * d)
    h = jnp.concatenate([x.reshape(batch * n_lat, d), ctx.reshape(batch * n_ctx, d)], axis=0)

    dr = e_k_k.shape[-1]
    low_rank = DECAY_RANK + ICLR_RANK
    head_ones = _block_diag(jnp.ones((dr // HEAD, HEAD, HEAD), F32))
    for i in range(depth):
        j = i // 2
        with_ctx = not (i == depth - 1 and i % 2 == 1)
        mods = mods_all[i]
        if not with_ctx:
            h = h[:tok.t_lat]
        h = _half_ffn(tok, h, mods, norm_g[i, 0], ffn1_w_gu[i].astype(BF16), ffn1_w_down[i].astype(BF16), final_g,
                      off=0, with_ctx=with_ctx)
        if i % 2 == 0:
            p = dict(w_in=e_w_in[j].astype(BF16), mu=e_mu[j], w0=e_w0[j], a0=e_a0[j], g_up=e_g_up[j],
                     w_up_pad=_pad_rows(e_w_up[j], 0, low_rank), a_up_pad=_pad_rows(e_a_up[j], DECAY_RANK, low_rank),
                     k_k=e_k_k[j], k_a=e_k_a[j], r_k=e_r_k[j], gn_w=e_gn_w[j], gn_b=e_gn_b[j],
                     w_out=e_w_out[j].astype(BF16), head_ones=head_ones.astype(BF16),
                     head_mean=(head_ones / HEAD).astype(BF16))
            h = _even_mixer(tok, h, mods, norm_g[i, 1], p, col_major=(j % 2 == 1))
        else:
            dc = o_conv_w.shape[-1]
            p = dict(conv_w=o_conv_w[j], conv_b=o_conv_b[j], cnorm_g=o_cnorm_g[j],
                     pool_w_bd=_block_diag(o_pool_w[j]).astype(BF16), pool_scale=o_pool_scale[j],
                     w_out_u=o_w_out[j, :dc].astype(BF16), w_out_p=o_w_out[j, dc:].astype(BF16))
            z = _mix_in(tok, h, mods, norm_g[i, 1], o_w_in[j].astype(BF16), with_ctx=with_ctx)
            h = _odd_mixer(tok, h, mods, z, p, with_ctx=with_ctx)
        h = _half_ffn(tok, h, mods, norm_g[i, 2], ffn2_w_gu[i].astype(BF16), ffn2_w_down[i].astype(BF16), final_g,
                      off=6, with_ctx=with_ctx, final=(i == depth - 1))
    return h[:tok.t_lat].reshape(batch, n_lat, d)
```

```python
import functools

import jax
import jax.numpy as jnp
import numpy as np
from jax import lax
from jax.experimental import pallas as pl
from jax.experimental.pallas import tpu as pltpu

F32 = jnp.float32
BF16 = jnp.bfloat16

N_MOD = 9
NORM_EPS = 1e-6
GRID_W = 64
HEAD = 64
PAIR = 2 * HEAD
DECAY_RANK = 64
ICLR_RANK = 64
GATE_RANK = 128
DECAY_SCALE = 0.606531
GN_EPS = 64e-5
FOURIER_GROUP = 64
CONV_WIDTH = 31
POOL_WIDTHS = (2, 4, 8, 16)
CHUNK = 64
HALO = 16
MOD_ROWS = 16
VMEM_LIMIT = 56 * 1024 * 1024


def _cparams(sem):
    return pltpu.CompilerParams(dimension_semantics=sem, vmem_limit_bytes=VMEM_LIMIT)


def _dot(a, b):
    return jnp.dot(a, b, preferred_element_type=F32)


def _dot_nt(a, b):
    return lax.dot_general(a, b, (((1,), (1,)), ((), ())), preferred_element_type=F32)


def _split2(x):
    hi = x.astype(BF16)
    lo = (x - hi.astype(F32)).astype(BF16)
    return hi, lo


def _dot_x2(x, w_bf16):
    hi, lo = _split2(x)
    return _dot(hi, w_bf16) + _dot(lo, w_bf16)


def _dot_x3(x, w):
    xh, xl = _split2(x)
    wh, wl = _split2(w)
    return _dot(xh, wh) + (_dot(xh, wl) + _dot(xl, wh))


def _sigmoid(x):
    return 1.0 / (1.0 + jnp.exp(-x))


def _silu(x):
    return x * _sigmoid(x)


def _modulate(h, g, shift, scale):
    y = h * lax.rsqrt(jnp.mean(h * h, axis=-1, keepdims=True) + NORM_EPS)
    return y * g * (1.0 + scale) + shift


def _ada_kernel(c_ref, w_ref, b_ref, o_ref):
    o_ref[0] = _dot_x3(_silu(c_ref[...]), w_ref[0]) + b_ref[0]


def _ada_table(cc, ada_w, ada_b):
    depth, d, nd = ada_w.shape
    tn = 1024
    return pl.pallas_call(
        _ada_kernel,
        grid=(depth, nd // tn),
        in_specs=[pl.BlockSpec((MOD_ROWS, d), lambda i, j: (0, 0)),
                  pl.BlockSpec((1, d, tn), lambda i, j: (i, 0, j)),
                  pl.BlockSpec((1, 1, tn), lambda i, j: (i, 0, j))],
        out_specs=pl.BlockSpec((1, MOD_ROWS, tn), lambda i, j: (i, 0, j)),
        out_shape=jax.ShapeDtypeStruct((depth, MOD_ROWS, nd), F32),
        compiler_params=_cparams(("parallel", "parallel")),
        name="ada_table",
    )(cc, ada_w, ada_b.reshape(depth, 1, nd))


class _Tokens:
    def __init__(self, batch, n_lat, n_ctx, d):
        self.batch, self.n_lat, self.n_ctx, self.d = batch, n_lat, n_ctx, d
        self.t_lat = batch * n_lat
        self.t_all = batch * (n_lat + n_ctx)

    def rows(self, with_ctx):
        return self.t_all if with_ctx else self.t_lat

    def mod_index(self, tm):
        lat_tiles, per_batch, ctx_row = self.t_lat // tm, self.n_lat // tm, self.batch
        return lambda t, *_: (jnp.where(t < lat_tiles, t // per_batch, ctx_row), 0, 0)


def _mod_slices(m_ref, off, d):
    return [m_ref[0, :, (off + k) * d:(off + k + 1) * d] for k in range(3)]


def _ffn_kernel(h_ref, m_ref, g_ref, wg_ref, wu_ref, wd_ref, fg_ref, o_ref, y_scr, acc_scr, *, off, final):
    f, nf, d = pl.program_id(1), pl.num_programs(1), h_ref.shape[1]
    shift, scale, gate = _mod_slices(m_ref, off, d)

    @pl.when(f == 0)
    def _():
        y_scr[...] = _modulate(h_ref[...], g_ref[...], shift, scale).astype(BF16)
        acc_scr[...] = jnp.zeros_like(acc_scr)

    y = y_scr[...]
    act = (_silu(_dot(y, wg_ref[...])) * _dot(y, wu_ref[...])).astype(BF16)
    acc_scr[...] += _dot(act, wd_ref[...])

    @pl.when(f == nf - 1)
    def _():
        out = h_ref[...] + (0.5 * gate) * acc_scr[...]
        if final:
            out = out * lax.rsqrt(jnp.mean(out * out, axis=-1, keepdims=True) + NORM_EPS) * fg_ref[...]
        o_ref[...] = out


def _half_ffn(tok, h, mods, g, w_gu, w_down, final_g, *, off, with_ctx, final=False):
    d, ff = tok.d, w_down.shape[0]
    tm, tf = 512, ff // 2
    nf = ff // tf
    rows = tok.rows(with_ctx)
    return pl.pallas_call(
        functools.partial(_ffn_kernel, off=off, final=final),
        grid=(rows // tm, nf),
        in_specs=[pl.BlockSpec((tm, d), lambda t, f: (t, 0)),
                  pl.BlockSpec((1, 1, N_MOD * d), tok.mod_index(tm)),
                  pl.BlockSpec((1, d), lambda t, f: (0, 0)),
                  pl.BlockSpec((d, tf), lambda t, f: (0, f)),
                  pl.BlockSpec((d, tf), lambda t, f: (0, nf + f)),
                  pl.BlockSpec((tf, d), lambda t, f: (f, 0)),
                  pl.BlockSpec((1, d), lambda t, f: (0, 0))],
        out_specs=pl.BlockSpec((tm, d), lambda t, f: (t, 0)),
        out_shape=jax.ShapeDtypeStruct((rows, d), F32),
        scratch_shapes=[pltpu.VMEM((tm, d), BF16), pltpu.VMEM((tm, d), F32)],
        compiler_params=_cparams(("parallel", "arbitrary")),
        name="half_ffn",
    )(h, mods, g.reshape(1, d), w_gu, w_gu, w_down, final_g.reshape(1, d))


def _mix_in_kernel(h_ref, m_ref, g_ref, w_ref, z_ref, y_scr):
    d = h_ref.shape[1]
    shift, scale, _ = _mod_slices(m_ref, 3, d)

    @pl.when(pl.program_id(1) == 0)
    def _():
        y_scr[...] = _modulate(h_ref[...], g_ref[...], shift, scale).astype(BF16)

    z_ref[...] = _dot(y_scr[...], w_ref[...])


def _mix_in(tok, h, mods, g, w_in, *, with_ctx):
    d, d_in = w_in.shape
    tm, tn = 512, d_in // 2
    rows = tok.rows(with_ctx)
    return pl.pallas_call(
        _mix_in_kernel,
        grid=(rows // tm, d_in // tn),
        in_specs=[pl.BlockSpec((tm, d), lambda t, j: (t, 0)),
                  pl.BlockSpec((1, 1, N_MOD * d), tok.mod_index(tm)),
                  pl.BlockSpec((1, d), lambda t, j: (0, 0)),
                  pl.BlockSpec((d, tn), lambda t, j: (0, j))],
        out_specs=pl.BlockSpec((tm, tn), lambda t, j: (t, j)),
        out_shape=jax.ShapeDtypeStruct((rows, d_in), F32),
        scratch_shapes=[pltpu.VMEM((tm, d), BF16)],
        compiler_params=_cparams(("parallel", "arbitrary")),
        name="mix_in",
    )(h, mods, g.reshape(1, d), w_in)


def _mix_out_kernel(h_ref, m_ref, x_ref, w_ref, o_ref):
    d = h_ref.shape[1]
    gate = m_ref[0, :, 5 * d:6 * d]
    o_ref[...] = h_ref[...] + gate * _dot(x_ref[...].astype(BF16), w_ref[...])


def _mix_out(tok, h, mods, x, w_out):
    d = tok.d
    tm = 512
    rows = x.shape[0]
    return pl.pallas_call(
        _mix_out_kernel,
        grid=(rows // tm,),
        in_specs=[pl.BlockSpec((tm, d), lambda t: (t, 0)),
                  pl.BlockSpec((1, 1, N_MOD * d), tok.mod_index(tm)),
                  pl.BlockSpec((tm, w_out.shape[0]), lambda t: (t, 0)),
                  pl.BlockSpec(w_out.shape, lambda t: (0, 0))],
        out_specs=pl.BlockSpec((tm, d), lambda t: (t, 0)),
        out_shape=jax.ShapeDtypeStruct((rows, d), F32),
        compiler_params=_cparams(("parallel",)),
        name="mix_out",
    )(h, mods, x, w_out)


def _feat_kernel(z_ref, zp_ref, zn_ref, mu_ref, wup_ref, aup_ref, gup_ref, w0_ref, a0_ref, kk_ref, ka_ref,
                 rk_ref, ones_ref, r_out, v_out, kap_out, lw_out, kd_out, b_out, g_out, bonus_out,
                 *, ctx_tiles, d_rwkv):
    i, last = pl.program_id(1), pl.num_programs(1) - 1
    tt = z_ref.shape[1]
    z = z_ref[0]
    first_of_seq = jnp.logical_or(i == 0, i == ctx_tiles)
    last_of_seq = jnp.logical_or(i == ctx_tiles - 1, i == last)
    prev_row = jnp.where(first_of_seq, 0.0, zp_ref[0, 7:8, :])
    next_row = jnp.where(last_of_seq, 0.0, zn_ref[0, 0:1, :])
    row = lax.broadcasted_iota(jnp.int32, (tt, 1), 0)
    z_prev = jnp.where(row == 0, prev_row, pltpu.roll(z, 1, axis=0))
    z_next = jnp.where(row == tt - 1, next_row, pltpu.roll(z, tt - 1, axis=0))
    z = z + mu_ref[...] * (0.5 * (z_prev + z_next) - z)

    dr = d_rwkv
    r, k, v = z[:, :dr], z[:, dr:2 * dr], z[:, 2 * dr:3 * dr]
    low = z[:, 3 * dr:3 * dr + DECAY_RANK + ICLR_RANK]
    gd = z[:, 3 * dr + DECAY_RANK + ICLR_RANK:]
    ones = ones_ref[...]

    kap = k * kk_ref[...]
    norm = jnp.sqrt(_dot_x2(kap * kap, ones))
    kap = kap / jnp.maximum(norm, 1e-12)
    tanh_low = jnp.tanh(low)
    kd_sum = jnp.zeros_like(k)
    for dd in range(2):
        lw_out[dd, 0] = -DECAY_SCALE * _sigmoid(w0_ref[dd] + _dot_x3(tanh_low, wup_ref[dd]))
        a = _sigmoid(a0_ref[dd] + _dot_x3(low, aup_ref[dd]))
        kd = k * (1.0 + (a - 1.0) * ka_ref[...])
        kd_out[dd, 0] = kd
        b_out[dd, 0] = kap * a
        kd_sum = kd_sum + kd
    r_out[0] = r
    v_out[0] = v
    kap_out[0] = kap
    g_out[0] = _dot_x3(_sigmoid(gd), gup_ref[...])
    bonus_out[0] = _dot_x2(r * kd_sum * rk_ref[...], ones) * v


def _rwkv_features(z_seq, n_ctx, p):
    batch, t, da = z_seq.shape
    dr = p["k_k"].shape[-1]
    tt = 256
    ctx_tiles = n_ctx // tt
    hb = tt // 8
    n_hblk = t // 8
    row = lambda name: p[name].reshape(1, dr)
    seq_spec = pl.BlockSpec((1, tt, dr), lambda b, i: (b, i, 0))
    dir_spec = pl.BlockSpec((2, 1, tt, dr), lambda b, i: (0, b, i, 0))
    full = lambda a: pl.BlockSpec(a.shape, lambda b, i: (0,) * a.ndim)
    seq_shape = jax.ShapeDtypeStruct((batch, t, dr), F32)
    dir_shape = jax.ShapeDtypeStruct((2, batch, t, dr), F32)
    consts = [p["mu"].reshape(1, da), p["w_up_pad"], p["a_up_pad"], p["g_up"],
              p["w0"].reshape(2, 1, dr), p["a0"].reshape(2, 1, dr), row("k_k"), row("k_a"), row("r_k"),
              p["head_ones"]]
    return pl.pallas_call(
        functools.partial(_feat_kernel, ctx_tiles=ctx_tiles, d_rwkv=dr),
        grid=(batch, t // tt),
        in_specs=[pl.BlockSpec((1, tt, da), lambda b, i: (b, i, 0)),
                  pl.BlockSpec((1, 8, da), lambda b, i: (b, jnp.maximum(i * hb - 1, 0), 0)),
                  pl.BlockSpec((1, 8, da), lambda b, i: (b, jnp.minimum((i + 1) * hb, n_hblk - 1), 0))]
                 + [full(a) for a in consts],
        out_specs=[seq_spec, seq_spec, seq_spec, dir_spec, dir_spec, dir_spec, seq_spec, seq_spec],
        out_shape=[seq_shape, seq_shape, seq_shape, dir_shape, dir_shape, dir_shape, seq_shape, seq_shape],
        compiler_params=_cparams(("parallel", "parallel")),
        name="rwkv_features",
    )(z_seq, z_seq, z_seq, *consts)


def _chunk_masks(ch, rev):
    ti = lax.broadcasted_iota(jnp.int32, (ch, ch), 0)
    tj = lax.broadcasted_iota(jnp.int32, (ch, ch), 1)
    ri = lax.broadcasted_iota(jnp.int32, (PAIR, PAIR), 0)
    ci = lax.broadcasted_iota(jnp.int32, (PAIR, PAIR), 1)
    same_head = (ri // ch) == (ci // ch)
    rt, ct = ri % ch, ci % ch
    before, upto = (ct > rt, ct >= rt) if rev else (ct < rt, ct <= rt)
    cum = (tj >= ti) if rev else (tj <= ti)
    return dict(cum=jnp.where(cum, 1.0, 0.0).astype(BF16), same_head=same_head, diag=ri == ci,
                strict=jnp.logical_and(same_head, before), incl=jnp.logical_and(same_head, upto))


def _scan_kernel(rf_ref, vf_ref, kapf_ref, lwf_ref, kdf_ref, bf_ref, rb_ref, vb_ref, kapb_ref, lwb_ref, kdb_ref,
                 bb_ref, yf_ref, yb_ref, s_scr):
    c = pl.program_id(1)
    ch = rf_ref.shape[1]
    n_pairs = rf_ref.shape[2] // PAIR

    @pl.when(c == 0)
    def _():
        s_scr[...] = jnp.zeros_like(s_scr)

    lane_head = lax.broadcasted_iota(jnp.int32, (ch, PAIR), 1) // HEAD
    head0, head1 = lane_head == 0, lane_head == 1

    def stack_heads(x):
        return jnp.concatenate([jnp.where(head0, x, 0.0), jnp.where(head1, x, 0.0)], axis=0)

    def fold_heads(x):
        return x[:ch] + x[ch:]

    chains = []
    dirs = ((rf_ref, vf_ref, kapf_ref, lwf_ref, kdf_ref, bf_ref, yf_ref),
            (rb_ref, vb_ref, kapb_ref, lwb_ref, kdb_ref, bb_ref, yb_ref))
    for d, (r_ref, v_ref, kap_ref, lw_ref, kd_ref, b_ref, y_ref) in enumerate(dirs):
        mk = _chunk_masks(ch, rev=(d == 1))
        lw_all = lw_ref[0, 0]
        l1 = lw_all.astype(BF16)
        rem = lw_all - l1.astype(F32)
        l2 = rem.astype(BF16)
        l3 = (rem - l2.astype(F32)).astype(BF16)
        cl_all = _dot(mk["cum"], l1) + (_dot(mk["cum"], l2) + _dot(mk["cum"], l3))
        for p in range(n_pairs):
            sl = slice(p * PAIR, (p + 1) * PAIR)
            chains.append(dict(d=d, p=p, sl=sl, mk=mk, y_ref=y_ref, lw=lw_all[:, sl], cl=cl_all[:, sl],
                               r=r_ref[0, :, sl], v=v_ref[0, :, sl], kap=kap_ref[0, :, sl],
                               kd=kd_ref[0, 0, :, sl], b=b_ref[0, 0, :, sl]))

    for q in chains:
        lw, cl, mk = q["lw"], q["cl"], q["mk"]
        tot = jnp.sum(lw, axis=0, keepdims=True)
        g_inv, g_rem = jnp.exp(-cl), jnp.exp(tot - cl)
        kq = stack_heads(q["kap"] * jnp.exp(cl - lw))
        rq = stack_heads(q["r"] * jnp.exp(cl))
        kdd, bdd = q["kd"] * g_inv, q["b"] * g_inv
        gram = _dot_nt(jnp.concatenate([kq, rq], axis=0).astype(BF16),
                       jnp.concatenate([bdd, bdd, kdd, kdd], axis=0).astype(BF16))
        q.update(kq=kq, rq=rq, g_tot=jnp.exp(tot),
                 kb=jnp.concatenate([q["kd"] * g_rem, q["b"] * g_rem], axis=0).astype(BF16),
                 l_b=jnp.where(mk["strict"], gram[:PAIR, :PAIR], 0.0),
                 l_k=jnp.where(mk["strict"], gram[:PAIR, PAIR:], 0.0),
                 pkb=jnp.concatenate([jnp.where(mk["incl"], gram[PAIR:, PAIR:], 0.0),
                                      jnp.where(mk["incl"], -gram[PAIR:, :PAIR], 0.0)], axis=1).astype(BF16))
    for q in chains:
        q["v2"] = jnp.concatenate([q["v"], q["v"]], axis=0).astype(BF16)
        lkv = jnp.where(q["mk"]["same_head"], _dot(q["l_k"].astype(BF16), q["v2"]), 0.0)
        q["x"] = jnp.concatenate([q["kq"], lkv], axis=1)
    for q in chains:
        q["lp"] = q["l_b"].astype(BF16)
        q["x"] = q["x"] - _dot(q["lp"], q["x"].astype(BF16))
    for _ in range(5):
        for q in chains:
            q["lp"] = _dot(q["lp"], q["lp"]).astype(BF16)
        for q in chains:
            q["x"] = q["x"] + _dot(q["lp"], q["x"].astype(BF16))
    for q in chains:
        mk = q["mk"]
        xk, w1 = q["x"][:, :PAIR], q["x"][:, PAIR:]
        rhs = jnp.concatenate([jnp.concatenate([q["v2"], jnp.zeros_like(q["v2"])], axis=1),
                               jnp.concatenate([w1, xk], axis=1).astype(BF16)], axis=0)
        big = _dot(q["pkb"], rhs)
        q["y0"] = fold_heads(jnp.where(mk["same_head"], big[:, :PAIR], 0.0))
        q["r_eff"] = (q["rq"] + big[:, PAIR:]).astype(BF16)
        lhs = jnp.concatenate([jnp.concatenate([q["v"], -fold_heads(w1)], axis=0),
                               jnp.concatenate([jnp.zeros_like(q["v"]), -fold_heads(xk)], axis=0)], axis=1)
        upd = _dot(lhs.T.astype(BF16), q["kb"])
        q["z0"] = jnp.where(mk["same_head"], upd[:PAIR], 0.0)
        q["a"] = jnp.where(mk["diag"], q["g_tot"], jnp.where(mk["same_head"], upd[PAIR:], 0.0)).astype(BF16)
    for q in chains:
        s0 = s_scr[q["d"], q["p"]].astype(BF16)
        q["y_ref"][0, 0, :, q["sl"]] = q["y0"] + fold_heads(_dot_nt(q["r_eff"], s0))
        s_scr[q["d"], q["p"]] = _dot(s0, q["a"]) + q["z0"]


def _rwkv_scan(r, v, kap, lw, kd, b, n_ctx):
    batch, t, dr = r.shape
    n_chunks, ctx_chunks = t // CHUNK, n_ctx // CHUNK

    def rev_chunk(c):
        return jnp.where(c < ctx_chunks, ctx_chunks - 1 - c, n_chunks + ctx_chunks - 1 - c)

    fwd_seq = pl.BlockSpec((1, CHUNK, dr), lambda bb, c: (bb, c, 0))
    bwd_seq = pl.BlockSpec((1, CHUNK, dr), lambda bb, c: (bb, rev_chunk(c), 0))
    fwd_dir = pl.BlockSpec((1, 1, CHUNK, dr), lambda bb, c: (0, bb, c, 0))
    bwd_dir = pl.BlockSpec((1, 1, CHUNK, dr), lambda bb, c: (1, bb, rev_chunk(c), 0))
    y_shape = jax.ShapeDtypeStruct((1, batch, t, dr), F32)
    return pl.pallas_call(
        _scan_kernel,
        grid=(batch, n_chunks),
        in_specs=[fwd_seq, fwd_seq, fwd_seq, fwd_dir, fwd_dir, fwd_dir,
                  bwd_seq, bwd_seq, bwd_seq, bwd_dir, bwd_dir, bwd_dir],
        out_specs=[fwd_dir, pl.BlockSpec((1, 1, CHUNK, dr), lambda bb, c: (0, bb, rev_chunk(c), 0))],
        out_shape=[y_shape, y_shape],
        scratch_shapes=[pltpu.VMEM((2, dr // PAIR, PAIR, PAIR), F32)],
        compiler_params=_cparams(("parallel", "arbitrary")),
        name="rwkv_scan",
    )(r, v, kap, lw, kd, b, r, v, kap, lw, kd, b)


def _rwkv_out_kernel(yf_ref, yb_ref, g_ref, bonus_ref, gnw_ref, gnb_ref, mean_ref, o_ref):
    y = yf_ref[0, 0] + yb_ref[0, 0]
    mean_mat = mean_ref[...]
    dev = y - _dot_x2(y, mean_mat)
    var = _dot_x2(dev * dev, mean_mat)
    yn = dev * lax.rsqrt(var + GN_EPS) * gnw_ref[...] + gnb_ref[...]
    o_ref[0] = (yn + bonus_ref[0]) * g_ref[0]


def _rwkv_out(y_fwd, y_bwd, g, bonus, p):
    batch, t, dr = g.shape
    tt = 256
    assert t % tt == 0
    seq_spec = pl.BlockSpec((1, tt, dr), lambda b, i: (b, i, 0))
    dir_spec = pl.BlockSpec((1, 1, tt, dr), lambda b, i: (0, b, i, 0))
    vec_spec = pl.BlockSpec((1, dr), lambda b, i: (0, 0))
    return pl.pallas_call(
        _rwkv_out_kernel,
        grid=(batch, t // tt),
        in_specs=[dir_spec, dir_spec, seq_spec, seq_spec, vec_spec, vec_spec,
                  pl.BlockSpec((dr, dr), lambda b, i: (0, 0))],
        out_specs=seq_spec,
        out_shape=jax.ShapeDtypeStruct((batch, t, dr), F32),
        compiler_params=_cparams(("parallel", "parallel")),
        name="rwkv_out",
    )(y_fwd, y_bwd, g, bonus, p["gn_w"].reshape(1, dr), p["gn_b"].reshape(1, dr), p["head_mean"])


def _fourier_kernel(cs_ref, u_ref, cg_ref, sg_ref, o_ref, stacked_scr):
    n = u_ref.shape[1]

    @pl.when(pl.program_id(1) == 0)
    def _():
        u_hi, u_lo = _split2(u_ref[0])
        stacked_scr[:n] = (_dot(u_hi, cg_ref[...]) + _dot(u_lo, cg_ref[...])).astype(BF16)
        stacked_scr[n:] = (_dot(u_hi, sg_ref[...]) + _dot(u_lo, sg_ref[...])).astype(BF16)

    o_ref[0] = _dot(cs_ref[...], stacked_scr[...])


def _dft_tables(n, groups):
    k = lax.broadcasted_iota(jnp.int32, (n, n), 0)
    m = lax.broadcasted_iota(jnp.int32, (n, n), 1)
    ang = ((k * m) % n).astype(F32) * (2.0 * np.pi / n)
    scale = 1.0 / np.sqrt(n * FOURIER_GROUP)
    cs = jnp.concatenate([jnp.cos(ang) * scale, -jnp.sin(ang) * scale], axis=1).astype(BF16)
    kg = np.arange(FOURIER_GROUP)
    ang_g = 2.0 * np.pi * ((kg[:, None] * kg[None, :]) % FOURIER_GROUP) / FOURIER_GROUP
    eye = np.eye(groups)
    cg = jnp.asarray(np.kron(eye, np.cos(ang_g)), F32).astype(BF16)
    sg = jnp.asarray(np.kron(eye, np.sin(ang_g)), F32).astype(BF16)
    return cs, cg, sg


def _fourier_mix(u):
    batch, n, ch = u.shape
    cs, cg, sg = _dft_tables(n, ch // FOURIER_GROUP)
    tn = min(n, 512)
    return pl.pallas_call(
        _fourier_kernel,
        grid=(batch, n // tn),
        in_specs=[pl.BlockSpec((tn, 2 * n), lambda b, i: (i, 0)),
                  pl.BlockSpec((1, n, ch), lambda b, i: (b, 0, 0)),
                  pl.BlockSpec((ch, ch), lambda b, i: (0, 0)),
                  pl.BlockSpec((ch, ch), lambda b, i: (0, 0))],
        out_specs=pl.BlockSpec((1, tn, ch), lambda b, i: (b, i, 0)),
        out_shape=jax.ShapeDtypeStruct((batch, n, ch), F32),
        scratch_shapes=[pltpu.VMEM((2 * n, ch), BF16)],
        compiler_params=_cparams(("parallel", "arbitrary")),
        name="fourier_mix",
    )(cs, u, cg, sg)


def _odd_kernel(h_ref, m_ref, z_ref, zp_ref, zn_ref, cw_ref, cb_ref, cg_ref, pw_ref, ps_ref, wu_ref, wp_ref,
                o_ref, u_scr, *, lat_tiles, lat_seq_tiles, n_lat, n_ctx, d_conv):
    i = pl.program_id(0)
    tt, d = h_ref.shape
    is_lat = i < lat_tiles
    seq_tile = jnp.where(is_lat, i % lat_seq_tiles, 0)
    seq_tiles = jnp.where(is_lat, lat_seq_tiles, n_ctx // tt)
    seq_len = jnp.where(is_lat, n_lat, n_ctx)
    zp = jnp.where(seq_tile == 0, 0.0, zp_ref[...])
    zn = jnp.where(seq_tile == seq_tiles - 1, 0.0, zn_ref[...])
    ext = jnp.concatenate([zp, z_ref[...], zn], axis=0)

    dc = d_conv
    u_scr[...] = ext[:, :dc] * _sigmoid(ext[:, dc:2 * dc])
    acc = jnp.zeros((tt, dc), F32) + cb_ref[...]
    for k in range(CONV_WIDTH):
        acc = acc + cw_ref[k:k + 1, :] * u_scr[pl.ds(HALO - CONV_WIDTH // 2 + k, tt), :]
    u = acc * lax.rsqrt(jnp.mean(acc * acc, axis=-1, keepdims=True) + NORM_EPS) * cg_ref[...]
    u = _silu(u)

    q = ext[:, 2 * dc:]
    rows = tt + 2 * HALO
    sums, win, step = [], q, 1
    for width in POOL_WIDTHS:
        if width == 2:
            win = q + pltpu.roll(q, 1, axis=0)
        else:
            win = pltpu.roll(win, step, axis=0) + pltpu.roll(win, rows - step, axis=0)
            step *= 2
        sums.append(win[HALO:HALO + tt])
    n_groups = len(POOL_WIDTHS)
    group = lax.broadcasted_iota(jnp.int32, (tt, q.shape[1]), 1) // (q.shape[1] // n_groups)
    pos = seq_tile * tt + lax.broadcasted_iota(jnp.int32, (tt, q.shape[1]), 0)
    total = sums[-1]
    width_of = jnp.full(group.shape, POOL_WIDTHS[-1], jnp.int32)
    for gi in range(n_groups - 2, -1, -1):
        total = jnp.where(group == gi, sums[gi], total)
        width_of = jnp.where(group == gi, POOL_WIDTHS[gi], width_of)
    lo = jnp.maximum(pos - width_of // 2, 0)
    hi = jnp.minimum(pos + (width_of - 1 - width_of // 2), seq_len - 1)
    pooled = total / (hi - lo + 1).astype(F32) - q[HALO:HALO + tt]
    pooled = _dot(pooled.astype(BF16), pw_ref[...]) * ps_ref[...]

    mixed = _dot(u.astype(BF16), wu_ref[...]) + _dot(pooled.astype(BF16), wp_ref[...])
    o_ref[...] = h_ref[...] + m_ref[0, :, 5 * d:6 * d] * mixed


def _odd_mixer(tok, h, mods, z, p, *, with_ctx):
    d, d_in = tok.d, z.shape[1]
    dc = p["conv_w"].shape[1]
    dp = d_in - 2 * dc
    tt = 256
    rows = tok.rows(with_ctx)
    hb, n_hblk = tt // HALO, z.shape[0] // HALO
    full = lambda a: pl.BlockSpec(a.shape, lambda i: (0,) * a.ndim)
    consts = [p["conv_w"], p["conv_b"].reshape(1, dc), p["cnorm_g"].reshape(1, dc), p["pool_w_bd"],
              p["pool_scale"].reshape(1, dp), p["w_out_u"], p["w_out_p"]]
    return pl.pallas_call(
        functools.partial(_odd_kernel, lat_tiles=tok.t_lat // tt, lat_seq_tiles=tok.n_lat // tt,
                          n_lat=tok.n_lat, n_ctx=tok.n_ctx, d_conv=dc),
        grid=(rows // tt,),
        in_specs=[pl.BlockSpec((tt, d), lambda i: (i, 0)),
                  pl.BlockSpec((1, 1, N_MOD * d), tok.mod_index(tt)),
                  pl.BlockSpec((tt, d_in), lambda i: (i, 0)),
                  pl.BlockSpec((HALO, d_in), lambda i: (jnp.maximum(i * hb - 1, 0), 0)),
                  pl.BlockSpec((HALO, d_in), lambda i: (jnp.minimum((i + 1) * hb, n_hblk - 1), 0))]
                 + [full(a) for a in consts],
        out_specs=pl.BlockSpec((tt, d), lambda i: (i, 0)),
        out_shape=jax.ShapeDtypeStruct((rows, d), F32),
        scratch_shapes=[pltpu.VMEM((tt + 2 * HALO, dc), F32)],
        compiler_params=_cparams(("parallel",)),
        name="odd_mixer",
    )(h, mods, z, z, z, *consts)


def _to_column_major(u):
    b, n, c = u.shape
    return u.reshape(b, n // GRID_W, GRID_W, c).transpose(0, 2, 1, 3).reshape(b, n, c)


def _from_column_major(u):
    b, n, c = u.shape
    return u.reshape(b, GRID_W, n // GRID_W, c).transpose(0, 2, 1, 3).reshape(b, n, c)


def _even_mixer(tok, h, mods, norm_g, p, col_major):
    z = _mix_in(tok, h, mods, norm_g, p["w_in"], with_ctx=True)
    b, n, nc = tok.batch, tok.n_lat, tok.n_ctx
    da = p["mu"].shape[-1]
    z_lat = z[:tok.t_lat].reshape(b, n, -1)
    z_ctx = z[tok.t_lat:].reshape(b, nc, -1)
    f_lat = _fourier_mix(z_lat[..., da:])
    f_ctx = _fourier_mix(z_ctx[..., da:])
    za_lat = z_lat[..., :da]
    if col_major:
        za_lat = _to_column_major(za_lat)
    z_seq = jnp.concatenate([z_ctx[..., :da], za_lat], axis=1)
    r, v, kap, lw, kd, bb, g, bonus = _rwkv_features(z_seq, nc, p)
    y_fwd, y_bwd = _rwkv_scan(r, v, kap, lw, kd, bb, nc)
    o = _rwkv_out(y_fwd, y_bwd, g, bonus, p)
    o_ctx, o_lat = o[:, :nc], o[:, nc:]
    if col_major:
        o_lat = _from_column_major(o_lat)
    x_lat = jnp.concatenate([o_lat, f_lat], axis=-1).reshape(tok.t_lat, -1)
    x_ctx = jnp.concatenate([o_ctx, f_ctx], axis=-1).reshape(b * nc, -1)
    return _mix_out(tok, h, mods, jnp.concatenate([x_lat, x_ctx], axis=0), p["w_out"])


def _pad_rows(w, before, total):
    return jnp.pad(w, ((0, 0), (before, total - before - w.shape[1]), (0, 0)))


def _block_diag(blocks):
    g, a, b = blocks.shape
    eye = jnp.eye(g, dtype=blocks.dtype)
    return (eye[:, None, :, None] * blocks[:, :, None, :]).reshape(g * a, g * b)


def kernel(x, c, ctx, c_ctx, ada_w, ada_b, norm_g, ffn1_w_gu, ffn1_w_down, ffn2_w_gu, ffn2_w_down,
           e_w_in, e_mu, e_w0, e_w_up, e_a0, e_a_up, e_g_up, e_k_k, e_k_a, e_r_k, e_gn_w, e_gn_b, e_w_out,
           o_w_in, o_conv_w, o_conv_b, o_cnorm_g, o_pool_w, o_pool_scale, o_w_out, final_g):
    batch, n_lat, d = x.shape
    n_ctx = ctx.shape[1]
    depth = ada_w.shape[0]
    tok = _Tokens(batch, n_lat, n_ctx, d)
    assert batch < MOD_ROWS and n_lat % 512 == 0 and (batch * n_ctx) % 512 == 0 and n_ctx % 256 == 0

    cc = jnp.zeros((MOD_ROWS, d), F32).at[:batch].set(c).at[batch].set(c_ctx)
    mods_all = _ada_table(cc, ada_w, ada_b).reshape(depth, MOD_ROWS, 1, N_MOD * d)
    h = jnp.concatenate([x.reshape(batch * n_lat, d), ctx.reshape(batch * n_ctx, d)], axis=0)

    dr = e_k_k.shape[-1]
    low_rank = DECAY_RANK + ICLR_RANK
    head_ones = _block_diag(jnp.ones((dr // HEAD, HEAD, HEAD), F32))
    for i in range(depth):
        j = i // 2
        with_ctx = not (i == depth - 1 and i % 2 == 1)
        mods = mods_all[i]
        if not with_ctx:
            h = h[:tok.t_lat]
        h = _half_ffn(tok, h, mods, norm_g[i, 0], ffn1_w_gu[i].astype(BF16), ffn1_w_down[i].astype(BF16), final_g,
                      off=0, with_ctx=with_ctx)
        if i % 2 == 0:
            p = dict(w_in=e_w_in[j].astype(BF16), mu=e_mu[j], w0=e_w0[j], a0=e_a0[j], g_up=e_g_up[j],
                     w_up_pad=_pad_rows(e_w_up[j], 0, low_rank), a_up_pad=_pad_rows(e_a_up[j], DECAY_RANK, low_rank),
                     k_k=e_k_k[j], k_a=e_k_a[j], r_k=e_r_k[j], gn_w=e_gn_w[j], gn_b=e_gn_b[j],
                     w_out=e_w_out[j].astype(BF16), head_ones=head_ones.astype(BF16),
                     head_mean=(head_ones / HEAD).astype(BF16))
            h = _even_mixer(tok, h, mods, norm_g[i, 1], p, col_major=(j % 2 == 1))
        else:
            dc = o_conv_w.shape[-1]
            p = dict(conv_w=o_conv_w[j], conv_b=o_conv_b[j], cnorm_g=o_cnorm_g[j],
                     pool_w_bd=_block_diag(o_pool_w[j]).astype(BF16), pool_scale=o_pool_scale[j],
                     w_out_u=o_w_out[j, :dc].astype(BF16), w_out_p=o_w_out[j, dc:].astype(BF16))
            z = _mix_in(tok, h, mods, norm_g[i, 1], o_w_in[j].astype(BF16), with_ctx=with_ctx)
            h = _odd_mixer(tok, h, mods, z, p, with_ctx=with_ctx)
        h = _half_ffn(tok, h, mods, norm_g[i, 2], ffn2_w_gu[i].astype(BF16), ffn2_w_down[i].astype(BF16), final_g,
                      off=6, with_ctx=with_ctx, final=(i == depth - 1))
    return h[:tok.t_lat].reshape(batch, n_lat, d)
```

```python
import functools

import jax
import jax.numpy as jnp
import numpy as np
from jax import lax
from jax.experimental import pallas as pl
from jax.experimental.pallas import tpu as pltpu

F32 = jnp.float32
BF16 = jnp.bfloat16

N_MOD = 9
NORM_EPS = 1e-6
GRID_W = 64
HEAD = 64
PAIR = 2 * HEAD
DECAY_RANK = 64
ICLR_RANK = 64
GATE_RANK = 128
DECAY_SCALE = 0.606531
GN_EPS = 64e-5
FOURIER_GROUP = 64
CONV_WIDTH = 31
POOL_WIDTHS = (2, 4, 8, 16)
CHUNK = 64
SCAN_CHUNKS = 2
HALO = 16
FEAT_TILE = 256
MOD_ROWS = 16
VMEM_LIMIT = 56 * 1024 * 1024


def _cparams(sem):
    return pltpu.CompilerParams(dimension_semantics=sem, vmem_limit_bytes=VMEM_LIMIT)


def _dot(a, b):
    return jnp.dot(a, b, preferred_element_type=F32)


def _dot_nt(a, b):
    return lax.dot_general(a, b, (((1,), (1,)), ((), ())), preferred_element_type=F32)


def _split2(x):
    hi = x.astype(BF16)
    lo = (x - hi.astype(F32)).astype(BF16)
    return hi, lo


def _dot_x2(x, w_bf16):
    hi, lo = _split2(x)
    return _dot(hi, w_bf16) + _dot(lo, w_bf16)


def _dot_x3(x, w):
    xh, xl = _split2(x)
    wh, wl = _split2(w)
    return _dot(xh, wh) + (_dot(xh, wl) + _dot(xl, wh))


def _sigmoid(x):
    return 1.0 / (1.0 + jnp.exp(-x))


def _silu(x):
    return x * _sigmoid(x)


def _modulate(h, g, shift, scale):
    y = h * lax.rsqrt(jnp.mean(h * h, axis=-1, keepdims=True) + NORM_EPS)
    return y * g * (1.0 + scale) + shift


def _ada_kernel(c_ref, w_ref, b_ref, o_ref):
    o_ref[0] = _dot_x3(_silu(c_ref[...]), w_ref[0]) + b_ref[0]


def _ada_table(cc, ada_w, ada_b):
    depth, d, nd = ada_w.shape
    tn = 1024
    return pl.pallas_call(
        _ada_kernel,
        grid=(depth, nd // tn),
        in_specs=[pl.BlockSpec((MOD_ROWS, d), lambda i, j: (0, 0)),
                  pl.BlockSpec((1, d, tn), lambda i, j: (i, 0, j)),
                  pl.BlockSpec((1, 1, tn), lambda i, j: (i, 0, j))],
        out_specs=pl.BlockSpec((1, MOD_ROWS, tn), lambda i, j: (i, 0, j)),
        out_shape=jax.ShapeDtypeStruct((depth, MOD_ROWS, nd), F32),
        compiler_params=_cparams(("parallel", "parallel")),
        name="ada_table",
    )(cc, ada_w, ada_b.reshape(depth, 1, nd))


class _Tokens:
    def __init__(self, batch, n_lat, n_ctx, d):
        self.batch, self.n_lat, self.n_ctx, self.d = batch, n_lat, n_ctx, d
        self.t_lat = batch * n_lat
        self.t_all = batch * (n_lat + n_ctx)

    def rows(self, with_ctx):
        return self.t_all if with_ctx else self.t_lat

    def mod_index(self, tm):
        lat_tiles, per_batch, ctx_row = self.t_lat // tm, self.n_lat // tm, self.batch
        return lambda t, *_: (jnp.where(t < lat_tiles, t // per_batch, ctx_row), 0, 0)


def _mod_slices(m_ref, off, d):
    return [m_ref[0, :, (off + k) * d:(off + k + 1) * d] for k in range(3)]


def _resident(shape):
    return pl.BlockSpec(shape, lambda *_: (0,) * len(shape), pipeline_mode=pl.Buffered(1))


def _ffn_kernel(h_ref, m_ref, g_ref, wgu_ref, wd_ref, fg_ref, o_ref, *, off, final):
    d, ff = h_ref.shape[1], wd_ref.shape[0]
    shift, scale, gate = _mod_slices(m_ref, off, d)
    h = h_ref[...]
    y = _modulate(h, g_ref[...], shift, scale).astype(BF16)
    gu = _dot(y, wgu_ref[...])
    act = (_silu(gu[:, :ff]) * gu[:, ff:]).astype(BF16)
    out = h + (0.5 * gate) * _dot(act, wd_ref[...])
    if final:
        out = out * lax.rsqrt(jnp.mean(out * out, axis=-1, keepdims=True) + NORM_EPS) * fg_ref[...]
    o_ref[...] = out


def _half_ffn(tok, h, mods, g, w_gu, w_down, final_g, *, off, with_ctx, final=False):
    d = tok.d
    tm = 512
    rows = tok.rows(with_ctx)
    return pl.pallas_call(
        functools.partial(_ffn_kernel, off=off, final=final),
        grid=(rows // tm,),
        in_specs=[pl.BlockSpec((tm, d), lambda t: (t, 0)),
                  pl.BlockSpec((1, 1, N_MOD * d), tok.mod_index(tm)),
                  pl.BlockSpec((1, d), lambda t: (0, 0)),
                  _resident(w_gu.shape),
                  _resident(w_down.shape),
                  pl.BlockSpec((1, d), lambda t: (0, 0))],
        out_specs=pl.BlockSpec((tm, d), lambda t: (t, 0)),
        out_shape=jax.ShapeDtypeStruct((rows, d), F32),
        compiler_params=_cparams(("parallel",)),
        name="half_ffn",
    )(h, mods, g.reshape(1, d), w_gu, w_down, final_g.reshape(1, d))


def _mix_in_kernel(h_ref, m_ref, g_ref, *refs):
    d = h_ref.shape[1]
    w_refs, z_refs = refs[:len(refs) // 2], refs[len(refs) // 2:]
    shift, scale, _ = _mod_slices(m_ref, 3, d)
    y = _modulate(h_ref[...], g_ref[...], shift, scale).astype(BF16)
    for w_ref, z_ref in zip(w_refs, z_refs):
        z_ref[...] = _dot(y, w_ref[...])


def _mix_in(tok, h, mods, g, weights, *, with_ctx):
    d = tok.d
    tm = 512
    rows = tok.rows(with_ctx)
    return pl.pallas_call(
        _mix_in_kernel,
        grid=(rows // tm,),
        in_specs=[pl.BlockSpec((tm, d), lambda t: (t, 0)),
                  pl.BlockSpec((1, 1, N_MOD * d), tok.mod_index(tm)),
                  pl.BlockSpec((1, d), lambda t: (0, 0))]
                 + [_resident(w.shape) for w in weights],
        out_specs=[pl.BlockSpec((tm, w.shape[1]), lambda t: (t, 0)) for w in weights],
        out_shape=[jax.ShapeDtypeStruct((rows, w.shape[1]), F32) for w in weights],
        compiler_params=_cparams(("parallel",)),
        name="mix_in",
    )(h, mods, g.reshape(1, d), *weights)


def _feat_kernel(*refs, col_major, lat_tiles, d_rwkv):
    if col_major:
        zt_ref, zc_ref, refs = refs[0], refs[1], refs[2:]
    else:
        zt_ref, refs = refs[0], refs[1:]
    (zp_ref, zn_ref, mu_ref, wup_ref, aup_ref, gup_ref, w0_ref, a0_ref, kk_ref, ka_ref, rk_ref, ones_ref,
     r_out, v_out, kap_out, lw_out, kd_out, b_out, g_out, bonus_out) = refs
    i = pl.program_id(1)
    tt, da = zt_ref.shape
    is_ctx = i == lat_tiles
    z = zt_ref[...]
    if col_major:
        cols = zc_ref.shape[1] // da
        z = jnp.where(is_ctx, z, jnp.concatenate([zc_ref[:, k * da:(k + 1) * da] for k in range(cols)], axis=0))
    first_of_seq = jnp.logical_or(i == 0, is_ctx)
    last_of_seq = jnp.logical_or(i == lat_tiles - 1, is_ctx)
    prev_row = jnp.where(first_of_seq, 0.0, zp_ref[7:8, :])
    next_row = jnp.where(last_of_seq, 0.0, zn_ref[0:1, :])
    row = lax.broadcasted_iota(jnp.int32, (tt, 1), 0)
    z_prev = jnp.where(row == 0, prev_row, pltpu.roll(z, 1, axis=0))
    z_next = jnp.where(row == tt - 1, next_row, pltpu.roll(z, tt - 1, axis=0))
    z = z + mu_ref[...] * (0.5 * (z_prev + z_next) - z)

    dr = d_rwkv
    r, k, v = z[:, :dr], z[:, dr:2 * dr], z[:, 2 * dr:3 * dr]
    low = z[:, 3 * dr:3 * dr + DECAY_RANK + ICLR_RANK]
    gd = z[:, 3 * dr + DECAY_RANK + ICLR_RANK:]
    ones = ones_ref[...]

    kap = k * kk_ref[...]
    norm = jnp.sqrt(_dot_x2(kap * kap, ones))
    kap = kap / jnp.maximum(norm, 1e-12)
    tanh_low = jnp.tanh(low)
    kd_sum = jnp.zeros_like(k)
    for dd in range(2):
        lw_out[dd, 0] = -DECAY_SCALE * _sigmoid(w0_ref[dd] + _dot_x3(tanh_low, wup_ref[dd]))
        a = _sigmoid(a0_ref[dd] + _dot_x3(low, aup_ref[dd]))
        kd = k * (1.0 + (a - 1.0) * ka_ref[...])
        kd_out[dd, 0] = kd.astype(kd_out.dtype)
        b_out[dd, 0] = (kap * a).astype(b_out.dtype)
        kd_sum = kd_sum + kd
    r_out[0] = r.astype(r_out.dtype)
    v_out[0] = v.astype(v_out.dtype)
    kap_out[0] = kap.astype(kap_out.dtype)
    g_out[0] = _dot_x3(_sigmoid(gd), gup_ref[...])
    bonus_out[0] = _dot_x2(r * kd_sum * rk_ref[...], ones) * v


def _rwkv_features(tok, za, p, col_major):
    batch, n, nc = tok.batch, tok.n_lat, tok.n_ctx
    t = n + nc
    da = za.shape[1]
    dr = p["k_k"].shape[-1]
    tt = FEAT_TILE
    lat_tiles = n // tt
    ctx_block = tok.t_lat // tt
    hb = tt // 8
    row = lambda name: p[name].reshape(1, dr)
    seq_spec = pl.BlockSpec((1, tt, dr), lambda b, i: (b, i, 0))
    dir_spec = pl.BlockSpec((2, 1, tt, dr), lambda b, i: (0, b, i, 0))
    full = lambda a: pl.BlockSpec(a.shape, lambda b, i: (0,) * a.ndim)
    consts = [p["mu"].reshape(1, da), p["w_up_pad"], p["a_up_pad"], p["g_up"],
              p["w0"].reshape(2, 1, dr), p["a0"].reshape(2, 1, dr), row("k_k"), row("k_a"), row("r_k"),
              p["head_ones"]]
    if col_major:
        n_rows = n // GRID_W
        cols = tt // n_rows
        zc = za.reshape(za.shape[0] // GRID_W, GRID_W * da)
        sub = n_rows // 8
        ins = [za, zc, zc, zc]
        in_specs = [pl.BlockSpec((tt, da), lambda b, i: (ctx_block + b, 0)),
                    pl.BlockSpec((n_rows, cols * da), lambda b, i: (b, jnp.minimum(i, lat_tiles - 1))),
                    pl.BlockSpec((8, da), lambda b, i: (b * sub + sub - 1, jnp.clip(i * cols - 1, 0, GRID_W - 1))),
                    pl.BlockSpec((8, da), lambda b, i: (b * sub, jnp.clip((i + 1) * cols, 0, GRID_W - 1)))]
    else:
        n_hblk = za.shape[0] // 8
        tile = lambda b, i: jnp.where(i < lat_tiles, b * lat_tiles + i, ctx_block + b)
        ins = [za, za, za]
        in_specs = [pl.BlockSpec((tt, da), lambda b, i: (tile(b, i), 0)),
                    pl.BlockSpec((8, da), lambda b, i: (jnp.maximum(tile(b, i) * hb - 1, 0), 0)),
                    pl.BlockSpec((8, da), lambda b, i: (jnp.minimum((tile(b, i) + 1) * hb, n_hblk - 1), 0))]
    seq = lambda dt: jax.ShapeDtypeStruct((batch, t, dr), dt)
    dirs = lambda dt: jax.ShapeDtypeStruct((2, batch, t, dr), dt)
    return pl.pallas_call(
        functools.partial(_feat_kernel, col_major=col_major, lat_tiles=lat_tiles, d_rwkv=dr),
        grid=(batch, t // tt),
        in_specs=in_specs + [full(a) for a in consts],
        out_specs=[seq_spec, seq_spec, seq_spec, dir_spec, dir_spec, dir_spec, seq_spec, seq_spec],
        out_shape=[seq(BF16), seq(BF16), seq(BF16), dirs(F32), dirs(BF16), dirs(BF16), seq(F32), seq(F32)],
        compiler_params=_cparams(("parallel", "parallel")),
        name="rwkv_features",
    )(*ins, *consts)


def _chunk_masks(ch, rev):
    ti = lax.broadcasted_iota(jnp.int32, (ch, ch), 0)
    tj = lax.broadcasted_iota(jnp.int32, (ch, ch), 1)
    ri = lax.broadcasted_iota(jnp.int32, (PAIR, PAIR), 0)
    ci = lax.broadcasted_iota(jnp.int32, (PAIR, PAIR), 1)
    same_head = (ri // ch) == (ci // ch)
    rt, ct = ri % ch, ci % ch
    before, upto = (ct > rt, ct >= rt) if rev else (ct < rt, ct <= rt)
    cum = (tj >= ti) if rev else (tj <= ti)
    return dict(cum=jnp.where(cum, 1.0, 0.0).astype(BF16), same_head=same_head, diag=ri == ci,
                strict=jnp.logical_and(same_head, before), incl=jnp.logical_and(same_head, upto))


def _scan_kernel(rf_ref, vf_ref, kapf_ref, lwf_ref, kdf_ref, bf_ref, rb_ref, vb_ref, kapb_ref, lwb_ref, kdb_ref,
                 bb_ref, yf_ref, yb_ref, s_scr):
    c = pl.program_id(1)
    ch = CHUNK
    n_sub = rf_ref.shape[1] // ch
    n_pairs = rf_ref.shape[2] // PAIR

    @pl.when(c == 0)
    def _():
        s_scr[...] = jnp.zeros_like(s_scr)

    lane_head = lax.broadcasted_iota(jnp.int32, (ch, PAIR), 1) // HEAD
    head0, head1 = lane_head == 0, lane_head == 1

    def stack_heads(x):
        return jnp.concatenate([jnp.where(head0, x, 0.0), jnp.where(head1, x, 0.0)], axis=0)

    def fold_heads(x):
        return x[:ch] + x[ch:]

    chains = []
    dirs = ((rf_ref, vf_ref, kapf_ref, lwf_ref, kdf_ref, bf_ref, yf_ref),
            (rb_ref, vb_ref, kapb_ref, lwb_ref, kdb_ref, bb_ref, yb_ref))
    for d, (r_ref, v_ref, kap_ref, lw_ref, kd_ref, b_ref, y_ref) in enumerate(dirs):
        mk = _chunk_masks(ch, rev=(d == 1))
        for sub in (range(n_sub) if d == 0 else reversed(range(n_sub))):
            rows = slice(sub * ch, (sub + 1) * ch)
            lw_all = lw_ref[0, 0, rows, :]
            l1 = lw_all.astype(BF16)
            rem = lw_all - l1.astype(F32)
            l2 = rem.astype(BF16)
            l3 = (rem - l2.astype(F32)).astype(BF16)
            cl_all = _dot(mk["cum"], l1) + (_dot(mk["cum"], l2) + _dot(mk["cum"], l3))
            for p in range(n_pairs):
                sl = slice(p * PAIR, (p + 1) * PAIR)
                f32 = lambda ref, *lead: ref[(*lead, rows, sl)].astype(F32)
                chains.append(dict(d=d, p=p, sl=sl, rows=rows, mk=mk, y_ref=y_ref, lw=lw_all[:, sl], cl=cl_all[:, sl],
                                   r=f32(r_ref, 0), v=f32(v_ref, 0), kap=f32(kap_ref, 0),
                                   kd=f32(kd_ref, 0, 0), b=f32(b_ref, 0, 0)))

    for q in chains:
        lw, cl, mk = q["lw"], q["cl"], q["mk"]
        tot = jnp.sum(lw, axis=0, keepdims=True)
        g_inv, g_rem = jnp.exp(-cl), jnp.exp(tot - cl)
        kq = stack_heads(q["kap"] * jnp.exp(cl - lw))
        rq = stack_heads(q["r"] * jnp.exp(cl))
        kdd, bdd = q["kd"] * g_inv, q["b"] * g_inv
        gram = _dot_nt(jnp.concatenate([kq, rq], axis=0).astype(BF16),
                       jnp.concatenate([bdd, bdd, kdd, kdd], axis=0).astype(BF16))
        q.update(kq=kq, rq=rq, g_tot=jnp.exp(tot),
                 kb=jnp.concatenate([q["kd"] * g_rem, q["b"] * g_rem], axis=0).astype(BF16),
                 l_b=jnp.where(mk["strict"], gram[:PAIR, :PAIR], 0.0),
                 l_k=jnp.where(mk["strict"], gram[:PAIR, PAIR:], 0.0),
                 pkb=jnp.concatenate([jnp.where(mk["incl"], gram[PAIR:, PAIR:], 0.0),
                                      jnp.where(mk["incl"], -gram[PAIR:, :PAIR], 0.0)], axis=1).astype(BF16))
    for q in chains:
        q["v2"] = jnp.concatenate([q["v"], q["v"]], axis=0).astype(BF16)
        lkv = jnp.where(q["mk"]["same_head"], _dot(q["l_k"].astype(BF16), q["v2"]), 0.0)
        q["x"] = jnp.concatenate([q["kq"], lkv], axis=1)
    for q in chains:
        q["lp"] = q["l_b"].astype(BF16)
        q["x"] = q["x"] - _dot(q["lp"], q["x"].astype(BF16))
    for _ in range(int(np.log2(ch)) - 1):
        for q in chains:
            q["lp"] = _dot(q["lp"], q["lp"]).astype(BF16)
        for q in chains:
            q["x"] = q["x"] + _dot(q["lp"], q["x"].astype(BF16))
    for q in chains:
        x = q["x"]
        mk = q["mk"]
        xk, w1 = x[:, :PAIR], x[:, PAIR:]
        rhs = jnp.concatenate([jnp.concatenate([q["v2"], jnp.zeros_like(q["v2"])], axis=1),
                               jnp.concatenate([w1, xk], axis=1).astype(BF16)], axis=0)
        big = _dot(q["pkb"], rhs)
        q["y0"] = fold_heads(jnp.where(mk["same_head"], big[:, :PAIR], 0.0))
        q["r_eff"] = (q["rq"] + big[:, PAIR:]).astype(BF16)
        lhs = jnp.concatenate([jnp.concatenate([q["v"], -fold_heads(w1)], axis=0),
                               jnp.concatenate([jnp.zeros_like(q["v"]), -fold_heads(xk)], axis=0)], axis=1)
        upd = _dot(lhs.T.astype(BF16), q["kb"])
        q["z0"] = jnp.where(mk["same_head"], upd[:PAIR], 0.0)
        q["a"] = jnp.where(mk["diag"], q["g_tot"], jnp.where(mk["same_head"], upd[PAIR:], 0.0)).astype(BF16)
    for q in chains:
        s0 = s_scr[q["d"], q["p"]].astype(BF16)
        q["y_ref"][0, q["rows"], q["sl"]] =q["y0"] + fold_heads(_dot_nt(q["r_eff"], s0))
        s_scr[q["d"], q["p"]] = _dot(s0, q["a"]) + q["z0"]


def _rwkv_scan(r, v, kap, lw, kd, b, n_ctx):
    batch, t, dr = r.shape
    rows = SCAN_CHUNKS * CHUNK
    assert t % rows == 0 and n_ctx % rows == 0
    n_blocks, ctx_blocks = t // rows, n_ctx // rows
    lat_blocks = n_blocks - ctx_blocks

    def fwd_block(c):
        return jnp.where(c < ctx_blocks, lat_blocks + c, c - ctx_blocks)

    def bwd_block(c):
        return n_blocks - 1 - c

    fwd_seq = pl.BlockSpec((1, rows, dr), lambda bb, c: (bb, fwd_block(c), 0))
    bwd_seq = pl.BlockSpec((1, rows, dr), lambda bb, c: (bb, bwd_block(c), 0))
    fwd_dir = pl.BlockSpec((1, 1, rows, dr), lambda bb, c: (0, bb, fwd_block(c), 0))
    bwd_dir = pl.BlockSpec((1, 1, rows, dr), lambda bb, c: (1, bb, bwd_block(c), 0))
    y_shape = jax.ShapeDtypeStruct((batch, t, dr), F32)
    return pl.pallas_call(
        _scan_kernel,
        grid=(batch, n_blocks),
        in_specs=[fwd_seq, fwd_seq, fwd_seq, fwd_dir, fwd_dir, fwd_dir,
                  bwd_seq, bwd_seq, bwd_seq, bwd_dir, bwd_dir, bwd_dir],
        out_specs=[fwd_seq, bwd_seq],
        out_shape=[y_shape, y_shape],
        scratch_shapes=[pltpu.VMEM((2, dr // PAIR, PAIR, PAIR), F32)],
        compiler_params=_cparams(("parallel", "arbitrary")),
        name="rwkv_scan",
    )(r, v, kap, lw, kd, b, r, v, kap, lw, kd, b)


def _even_out_kernel(*refs, col_major, lat_tiles):
    n_seq = 4
    h_ref, m_ref = refs[:2]
    seq_refs = refs[2:2 + n_seq * (2 if col_major else 1)]
    fl_ref, fc_ref, gnw_ref, gnb_ref, mean_ref, wo_ref, wf_ref, o_ref = refs[2 + len(seq_refs):]
    t = pl.program_id(0)
    d = h_ref.shape[1]
    is_lat = t < lat_tiles

    def tile(k):
        if not col_major:
            return seq_refs[k][0]
        cm_ref, ctx_ref = seq_refs[2 * k], seq_refs[2 * k + 1]
        dr = ctx_ref.shape[2]
        rows = cm_ref.shape[2] // dr
        lat = jnp.concatenate([cm_ref[0, :, j * dr:(j + 1) * dr] for j in range(rows)], axis=0)
        return jnp.where(is_lat, lat, ctx_ref[0])

    y, g, bonus = tile(0) + tile(1), tile(2), tile(3)
    mean_mat = mean_ref[...]
    dev = y - _dot_x2(y, mean_mat)
    var = _dot_x2(dev * dev, mean_mat)
    yn = dev * lax.rsqrt(var + GN_EPS) * gnw_ref[...] + gnb_ref[...]
    o = ((yn + bonus) * g).astype(BF16)
    f = jnp.where(is_lat, fl_ref[...], fc_ref[...]).astype(BF16)
    mixed = _dot(o, wo_ref[...]) + _dot(f, wf_ref[...])
    o_ref[...] = h_ref[...] + m_ref[0, :, 5 * d:6 * d] * mixed


def _even_out(tok, h, mods, seq_arrays, f_lat, f_ctx, p, col_major):
    d, batch, n, nc = tok.d, tok.batch, tok.n_lat, tok.n_ctx
    dr = p["gn_w"].shape[-1]
    tt = FEAT_TILE
    per_batch = n // tt
    lat_tiles = batch * per_batch
    ctx_blk = n // tt
    lat_b = lambda t: jnp.minimum(t, lat_tiles - 1) // per_batch
    lat_i = lambda t: jnp.minimum(t, lat_tiles - 1) % per_batch
    ctx_b = lambda t: jnp.maximum(t - lat_tiles, 0)
    if col_major:
        n_rows = n // GRID_W
        rows = tt // GRID_W
        cm_spec = pl.BlockSpec((1, GRID_W, rows * dr), lambda t: (lat_b(t), 0, lat_i(t)))
        ctx_spec = pl.BlockSpec((1, tt, dr), lambda t: (ctx_b(t), ctx_blk, 0))
        seq_ins, seq_specs = [], []
        for a in seq_arrays:
            seq_ins += [a.reshape(batch, a.shape[1] // n_rows, n_rows * dr), a]
            seq_specs += [cm_spec, ctx_spec]
    else:
        seq_ins = list(seq_arrays)
        seq_specs = [pl.BlockSpec((1, tt, dr), lambda t: (jnp.where(t < lat_tiles, lat_b(t), ctx_b(t)),
                                                          jnp.where(t < lat_tiles, lat_i(t), ctx_blk), 0))] * len(seq_ins)
    df = f_lat.shape[-1]
    vec = lambda a: pl.BlockSpec((1, a.shape[-1]), lambda t: (0, 0))
    return pl.pallas_call(
        functools.partial(_even_out_kernel, col_major=col_major, lat_tiles=lat_tiles),
        grid=(tok.t_all // tt,),
        in_specs=[pl.BlockSpec((tt, d), lambda t: (t, 0)),
                  pl.BlockSpec((1, 1, N_MOD * d), tok.mod_index(tt))]
                 + seq_specs
                 + [pl.BlockSpec((tt, df), lambda t: (jnp.minimum(t, lat_tiles - 1), 0)),
                    pl.BlockSpec((tt, df), lambda t: (ctx_b(t), 0)),
                    vec(p["gn_w"]), vec(p["gn_b"]), _resident(p["head_mean"].shape),
                    _resident(p["w_out_o"].shape), _resident(p["w_out_f"].shape)],
        out_specs=pl.BlockSpec((tt, d), lambda t: (t, 0)),
        out_shape=jax.ShapeDtypeStruct((tok.t_all, d), F32),
        compiler_params=_cparams(("parallel",)),
        name="even_out",
    )(h, mods, *seq_ins, f_lat.reshape(-1, df), f_ctx.reshape(-1, df), p["gn_w"].reshape(1, dr),
      p["gn_b"].reshape(1, dr), p["head_mean"], p["w_out_o"], p["w_out_f"])


def _fourier_kernel(cs_ref, u_ref, cg_ref, sg_ref, o_ref, stacked_scr):
    n = u_ref.shape[0]

    @pl.when(pl.program_id(1) == 0)
    def _():
        u_hi, u_lo = _split2(u_ref[...])
        stacked_scr[:n] = (_dot(u_hi, cg_ref[...]) + _dot(u_lo, cg_ref[...])).astype(BF16)
        stacked_scr[n:] = (_dot(u_hi, sg_ref[...]) + _dot(u_lo, sg_ref[...])).astype(BF16)

    o_ref[0] = _dot(cs_ref[...], stacked_scr[...])


def _dft_tables(n, groups):
    base = FOURIER_GROUP
    k = jnp.arange(n, dtype=jnp.int32)[:, None]
    ang_hi = ((k * (jnp.arange(n // base, dtype=jnp.int32) * base)[None, :]) % n).astype(F32) * (2.0 * np.pi / n)
    ang_lo = ((k * jnp.arange(base, dtype=jnp.int32)[None, :]) % n).astype(F32) * (2.0 * np.pi / n)
    scale = 1.0 / np.sqrt(n * FOURIER_GROUP)
    c_hi, s_hi = (jnp.cos(ang_hi) * scale)[:, :, None], (jnp.sin(ang_hi) * scale)[:, :, None]
    c_lo, s_lo = jnp.cos(ang_lo)[:, None, :], jnp.sin(ang_lo)[:, None, :]
    cos = (c_hi * c_lo - s_hi * s_lo).reshape(n, n)
    sin = (s_hi * c_lo + c_hi * s_lo).reshape(n, n)
    cs = jnp.concatenate([cos, -sin], axis=1).astype(BF16)
    kg = np.arange(FOURIER_GROUP)
    ang_g = 2.0 * np.pi * ((kg[:, None] * kg[None, :]) % FOURIER_GROUP) / FOURIER_GROUP
    eye = np.eye(groups)
    cg = jnp.asarray(np.kron(eye, np.cos(ang_g)), F32).astype(BF16)
    sg = jnp.asarray(np.kron(eye, np.sin(ang_g)), F32).astype(BF16)
    return cs, cg, sg


def _fourier_mix(zf, batch, n, first_block):
    ch = zf.shape[1]
    cs, cg, sg = _dft_tables(n, ch // FOURIER_GROUP)
    tn = min(n, 512)
    return pl.pallas_call(
        _fourier_kernel,
        grid=(batch, n // tn),
        in_specs=[pl.BlockSpec((tn, 2 * n), lambda b, i: (i, 0)),
                  pl.BlockSpec((n, ch), lambda b, i: (first_block + b, 0)),
                  pl.BlockSpec((ch, ch), lambda b, i: (0, 0)),
                  pl.BlockSpec((ch, ch), lambda b, i: (0, 0))],
        out_specs=pl.BlockSpec((1, tn, ch), lambda b, i: (b, i, 0)),
        out_shape=jax.ShapeDtypeStruct((batch, n, ch), F32),
        scratch_shapes=[pltpu.VMEM((2 * n, ch), BF16)],
        compiler_params=_cparams(("parallel", "arbitrary")),
        name="fourier_mix",
    )(cs, zf, cg, sg)


def _odd_kernel(h_ref, m_ref, z_ref, zp_ref, zn_ref, cw_ref, cb_ref, cg_ref, pw_ref, ps_ref, wu_ref, wp_ref,
                o_ref, u_scr, *, lat_tiles, lat_seq_tiles, n_lat, n_ctx, d_conv):
    i = pl.program_id(0)
    tt, d = h_ref.shape
    is_lat = i < lat_tiles
    seq_tile = jnp.where(is_lat, i % lat_seq_tiles, 0)
    seq_tiles = jnp.where(is_lat, lat_seq_tiles, n_ctx // tt)
    seq_len = jnp.where(is_lat, n_lat, n_ctx)
    zp = jnp.where(seq_tile == 0, 0.0, zp_ref[...])
    zn = jnp.where(seq_tile == seq_tiles - 1, 0.0, zn_ref[...])
    ext = jnp.concatenate([zp, z_ref[...], zn], axis=0)

    dc = d_conv
    u_scr[...] = ext[:, :dc] * _sigmoid(ext[:, dc:2 * dc])
    acc = jnp.zeros((tt, dc), F32) + cb_ref[...]
    for k in range(CONV_WIDTH):
        acc = acc + cw_ref[k:k + 1, :] * u_scr[pl.ds(HALO - CONV_WIDTH // 2 + k, tt), :]
    u = acc * lax.rsqrt(jnp.mean(acc * acc, axis=-1, keepdims=True) + NORM_EPS) * cg_ref[...]
    u = _silu(u)

    q = ext[:, 2 * dc:]
    rows = tt + 2 * HALO
    sums, win, step = [], q, 1
    for width in POOL_WIDTHS:
        if width == 2:
            win = q + pltpu.roll(q, 1, axis=0)
        else:
            win = pltpu.roll(win, step, axis=0) + pltpu.roll(win, rows - step, axis=0)
            step *= 2
        sums.append(win[HALO:HALO + tt])
    n_groups = len(POOL_WIDTHS)
    group = lax.broadcasted_iota(jnp.int32, (tt, q.shape[1]), 1) // (q.shape[1] // n_groups)
    pos = seq_tile * tt + lax.broadcasted_iota(jnp.int32, (tt, q.shape[1]), 0)
    total = sums[-1]
    width_of = jnp.full(group.shape, POOL_WIDTHS[-1], jnp.int32)
    for gi in range(n_groups - 2, -1, -1):
        total = jnp.where(group == gi, sums[gi], total)
        width_of = jnp.where(group == gi, POOL_WIDTHS[gi], width_of)
    lo = jnp.maximum(pos - width_of // 2, 0)
    hi = jnp.minimum(pos + (width_of - 1 - width_of // 2), seq_len - 1)
    pooled = total / (hi - lo + 1).astype(F32) - q[HALO:HALO + tt]
    pooled = _dot(pooled.astype(BF16), pw_ref[...]) * ps_ref[...]

    mixed = _dot(u.astype(BF16), wu_ref[...]) + _dot(pooled.astype(BF16), wp_ref[...])
    o_ref[...] = h_ref[...] + m_ref[0, :, 5 * d:6 * d] * mixed


def _odd_mixer(tok, h, mods, z, p, *, with_ctx):
    d, d_in = tok.d, z.shape[1]
    dc = p["conv_w"].shape[1]
    dp = d_in - 2 * dc
    tt = 256
    rows = tok.rows(with_ctx)
    hb, n_hblk = tt // HALO, z.shape[0] // HALO
    full = lambda a: pl.BlockSpec(a.shape, lambda i: (0,) * a.ndim)
    consts = [p["conv_w"], p["conv_b"].reshape(1, dc), p["cnorm_g"].reshape(1, dc), p["pool_w_bd"],
              p["pool_scale"].reshape(1, dp), p["w_out_u"], p["w_out_p"]]
    return pl.pallas_call(
        functools.partial(_odd_kernel, lat_tiles=tok.t_lat // tt, lat_seq_tiles=tok.n_lat // tt,
                          n_lat=tok.n_lat, n_ctx=tok.n_ctx, d_conv=dc),
        grid=(rows // tt,),
        in_specs=[pl.BlockSpec((tt, d), lambda i: (i, 0)),
                  pl.BlockSpec((1, 1, N_MOD * d), tok.mod_index(tt)),
                  pl.BlockSpec((tt, d_in), lambda i: (i, 0)),
                  pl.BlockSpec((HALO, d_in), lambda i: (jnp.maximum(i * hb - 1, 0), 0)),
                  pl.BlockSpec((HALO, d_in), lambda i: (jnp.minimum((i + 1) * hb, n_hblk - 1), 0))]
                 + [full(a) for a in consts],
        out_specs=pl.BlockSpec((tt, d), lambda i: (i, 0)),
        out_shape=jax.ShapeDtypeStruct((rows, d), F32),
        scratch_shapes=[pltpu.VMEM((tt + 2 * HALO, dc), F32)],
        compiler_params=_cparams(("parallel",)),
        name="odd_mixer",
    )(h, mods, z, z, z, *consts)


def _even_mixer(tok, h, mods, norm_g, p, col_major):
    za, zf = _mix_in(tok, h, mods, norm_g, [p["w_in_a"], p["w_in_f"]], with_ctx=True)
    f_lat = _fourier_mix(zf, tok.batch, tok.n_lat, 0)
    f_ctx = _fourier_mix(zf, tok.batch, tok.n_ctx, tok.t_lat // tok.n_ctx)
    r, v, kap, lw, kd, bb, g, bonus = _rwkv_features(tok, za, p, col_major)
    y_fwd, y_bwd = _rwkv_scan(r, v, kap, lw, kd, bb, tok.n_ctx)
    return _even_out(tok, h, mods, [y_fwd, y_bwd, g, bonus], f_lat, f_ctx, p, col_major)


def _pad_rows(w, before, total):
    return jnp.pad(w, ((0, 0), (before, total - before - w.shape[1]), (0, 0)))


def _block_diag(blocks):
    g, a, b = blocks.shape
    eye = jnp.eye(g, dtype=blocks.dtype)
    return (eye[:, None, :, None] * blocks[:, :, None, :]).reshape(g * a, g * b)


def kernel(x, c, ctx, c_ctx, ada_w, ada_b, norm_g, ffn1_w_gu, ffn1_w_down, ffn2_w_gu, ffn2_w_down,
           e_w_in, e_mu, e_w0, e_w_up, e_a0, e_a_up, e_g_up, e_k_k, e_k_a, e_r_k, e_gn_w, e_gn_b, e_w_out,
           o_w_in, o_conv_w, o_conv_b, o_cnorm_g, o_pool_w, o_pool_scale, o_w_out, final_g):
    batch, n_lat, d = x.shape
    n_ctx = ctx.shape[1]
    depth = ada_w.shape[0]
    tok = _Tokens(batch, n_lat, n_ctx, d)
    assert batch < MOD_ROWS and n_lat % 512 == 0 and (batch * n_ctx) % 512 == 0 and n_ctx == FEAT_TILE
    assert n_lat % (GRID_W * 8) == 0 and FEAT_TILE % (n_lat // GRID_W) == 0 and FEAT_TILE % GRID_W == 0

    cc = jnp.zeros((MOD_ROWS, d), F32).at[:batch].set(c).at[batch].set(c_ctx)
    mods_all = _ada_table(cc, ada_w, ada_b).reshape(depth, MOD_ROWS, 1, N_MOD * d)
    h = jnp.concatenate([x.reshape(batch * n_lat, d), ctx.reshape(batch * n_ctx, d)], axis=0)

    dr = e_k_k.shape[-1]
    low_rank = DECAY_RANK + ICLR_RANK
    head_ones = _block_diag(jnp.ones((dr // HEAD, HEAD, HEAD), F32))
    for i in range(depth):
        j = i // 2
        with_ctx = not (i == depth - 1 and i % 2 == 1)
        mods = mods_all[i]
        h = _half_ffn(tok, h, mods, norm_g[i, 0], ffn1_w_gu[i].astype(BF16), ffn1_w_down[i].astype(BF16), final_g,
                      off=0, with_ctx=with_ctx)
        if i % 2 == 0:
            da = e_mu.shape[-1]
            p = dict(w_in_a=e_w_in[j, :, :da].astype(BF16), w_in_f=e_w_in[j, :, da:].astype(BF16),
                     mu=e_mu[j], w0=e_w0[j], a0=e_a0[j], g_up=e_g_up[j],
                     w_up_pad=_pad_rows(e_w_up[j], 0, low_rank), a_up_pad=_pad_rows(e_a_up[j], DECAY_RANK, low_rank),
                     k_k=e_k_k[j], k_a=e_k_a[j], r_k=e_r_k[j], gn_w=e_gn_w[j], gn_b=e_gn_b[j],
                     w_out_o=e_w_out[j, :dr].astype(BF16), w_out_f=e_w_out[j, dr:].astype(BF16),
                     head_ones=head_ones.astype(BF16),
                     head_mean=(head_ones / HEAD).astype(BF16))
            h = _even_mixer(tok, h, mods, norm_g[i, 1], p, col_major=(j % 2 == 1))
        else:
            dc = o_conv_w.shape[-1]
            p = dict(conv_w=o_conv_w[j], conv_b=o_conv_b[j], cnorm_g=o_cnorm_g[j],
                     pool_w_bd=_block_diag(o_pool_w[j]).astype(BF16), pool_scale=o_pool_scale[j],
                     w_out_u=o_w_out[j, :dc].astype(BF16), w_out_p=o_w_out[j, dc:].astype(BF16))
            (z,) = _mix_in(tok, h, mods, norm_g[i, 1], [o_w_in[j].astype(BF16)], with_ctx=with_ctx)
            h = _odd_mixer(tok, h, mods, z, p, with_ctx=with_ctx)
        h = _half_ffn(tok, h, mods, norm_g[i, 2], ffn2_w_gu[i].astype(BF16), ffn2_w_down[i].astype(BF16), final_g,
                      off=6, with_ctx=with_ctx, final=(i == depth - 1))
    return h[:tok.t_lat].reshape(batch, n_lat, d)
```

```python
import functools

import jax
import jax.numpy as jnp
import numpy as np
from jax import lax
from jax.experimental import pallas as pl
from jax.experimental.pallas import tpu as pltpu

F32 = jnp.float32
BF16 = jnp.bfloat16

N_MOD = 9
NORM_EPS = 1e-6
GRID_W = 64
HEAD = 64
PAIR = 2 * HEAD
DECAY_RANK = 64
ICLR_RANK = 64
GATE_RANK = 128
DECAY_SCALE = 0.606531
GN_EPS = 64e-5
FOURIER_GROUP = 64
CONV_WIDTH = 31
POOL_WIDTHS = (2, 4, 8, 16)
CHUNK = 64
SCAN_CHUNKS = 2
HALO = 16
FEAT_TILE = 256
MOD_ROWS = 16
VMEM_LIMIT = 56 * 1024 * 1024


def _cparams(sem):
    return pltpu.CompilerParams(dimension_semantics=sem, vmem_limit_bytes=VMEM_LIMIT)


def _dot(a, b):
    return jnp.dot(a, b, preferred_element_type=F32)


def _dot_nt(a, b):
    return lax.dot_general(a, b, (((1,), (1,)), ((), ())), preferred_element_type=F32)


def _split2(x):
    hi = x.astype(BF16)
    lo = (x - hi.astype(F32)).astype(BF16)
    return hi, lo


def _dot_x2(x, w_bf16):
    hi, lo = _split2(x)
    return _dot(hi, w_bf16) + _dot(lo, w_bf16)


def _dot_x3(x, w):
    xh, xl = _split2(x)
    wh, wl = _split2(w)
    return _dot(xh, wh) + (_dot(xh, wl) + _dot(xl, wh))


def _sigmoid(x):
    return 1.0 / (1.0 + jnp.exp(-x))


def _silu(x):
    return x * _sigmoid(x)


def _modulate(h, g, shift, scale):
    y = h * lax.rsqrt(jnp.mean(h * h, axis=-1, keepdims=True) + NORM_EPS)
    return y * g * (1.0 + scale) + shift


def _ada_kernel(c_ref, w_ref, b_ref, o_ref):
    o_ref[0] = _dot_x3(_silu(c_ref[...]), w_ref[0]) + b_ref[0]


def _ada_table(cc, ada_w, ada_b):
    depth, d, nd = ada_w.shape
    tn = 1024
    return pl.pallas_call(
        _ada_kernel,
        grid=(depth, nd // tn),
        in_specs=[pl.BlockSpec((MOD_ROWS, d), lambda i, j: (0, 0)),
                  pl.BlockSpec((1, d, tn), lambda i, j: (i, 0, j)),
                  pl.BlockSpec((1, 1, tn), lambda i, j: (i, 0, j))],
        out_specs=pl.BlockSpec((1, MOD_ROWS, tn), lambda i, j: (i, 0, j)),
        out_shape=jax.ShapeDtypeStruct((depth, MOD_ROWS, nd), F32),
        compiler_params=_cparams(("parallel", "parallel")),
        name="ada_table",
    )(cc, ada_w, ada_b.reshape(depth, 1, nd))


class _Tokens:
    def __init__(self, batch, n_lat, n_ctx, d):
        self.batch, self.n_lat, self.n_ctx, self.d = batch, n_lat, n_ctx, d
        self.t_lat = batch * n_lat
        self.t_all = batch * (n_lat + n_ctx)

    def rows(self, with_ctx):
        return self.t_all if with_ctx else self.t_lat

    def mod_index(self, tm):
        lat_tiles, per_batch, ctx_row = self.t_lat // tm, self.n_lat // tm, self.batch
        return lambda t, *_: (jnp.where(t < lat_tiles, t // per_batch, ctx_row), 0, 0)


def _mod_slices(m_ref, off, d):
    return [m_ref[0, :, (off + k) * d:(off + k + 1) * d] for k in range(3)]


def _resident(shape):
    return pl.BlockSpec(shape, lambda *_: (0,) * len(shape), pipeline_mode=pl.Buffered(1))


def _ffn_kernel(h_ref, m_ref, g_ref, wgu_ref, wd_ref, fg_ref, o_ref, *, off, final):
    d, ff = h_ref.shape[1], wd_ref.shape[0]
    shift, scale, gate = _mod_slices(m_ref, off, d)
    h = h_ref[...]
    y = _modulate(h, g_ref[...], shift, scale).astype(BF16)
    gu = _dot(y, wgu_ref[...])
    act = (_silu(gu[:, :ff]) * gu[:, ff:]).astype(BF16)
    out = h + (0.5 * gate) * _dot(act, wd_ref[...])
    if final:
        out = out * lax.rsqrt(jnp.mean(out * out, axis=-1, keepdims=True) + NORM_EPS) * fg_ref[...]
    o_ref[...] = out


def _half_ffn(tok, h, mods, g, w_gu, w_down, final_g, *, off, with_ctx, final=False):
    d = tok.d
    tm = 512
    rows = tok.rows(with_ctx)
    return pl.pallas_call(
        functools.partial(_ffn_kernel, off=off, final=final),
        grid=(rows // tm,),
        in_specs=[pl.BlockSpec((tm, d), lambda t: (t, 0)),
                  pl.BlockSpec((1, 1, N_MOD * d), tok.mod_index(tm)),
                  pl.BlockSpec((1, d), lambda t: (0, 0)),
                  _resident(w_gu.shape),
                  _resident(w_down.shape),
                  pl.BlockSpec((1, d), lambda t: (0, 0))],
        out_specs=pl.BlockSpec((tm, d), lambda t: (t, 0)),
        out_shape=jax.ShapeDtypeStruct((rows, d), F32),
        compiler_params=_cparams(("parallel",)),
        name="half_ffn",
    )(h, mods, g.reshape(1, d), w_gu, w_down, final_g.reshape(1, d))


def _mix_in_kernel(h_ref, m_ref, g_ref, *refs):
    d = h_ref.shape[1]
    w_refs, z_refs = refs[:len(refs) // 2], refs[len(refs) // 2:]
    shift, scale, _ = _mod_slices(m_ref, 3, d)
    y = _modulate(h_ref[...], g_ref[...], shift, scale).astype(BF16)
    for w_ref, z_ref in zip(w_refs, z_refs):
        z_ref[...] = _dot(y, w_ref[...])


def _mix_in(tok, h, mods, g, weights, *, with_ctx):
    d = tok.d
    tm = 512
    rows = tok.rows(with_ctx)
    return pl.pallas_call(
        _mix_in_kernel,
        grid=(rows // tm,),
        in_specs=[pl.BlockSpec((tm, d), lambda t: (t, 0)),
                  pl.BlockSpec((1, 1, N_MOD * d), tok.mod_index(tm)),
                  pl.BlockSpec((1, d), lambda t: (0, 0))]
                 + [_resident(w.shape) for w in weights],
        out_specs=[pl.BlockSpec((tm, w.shape[1]), lambda t: (t, 0)) for w in weights],
        out_shape=[jax.ShapeDtypeStruct((rows, w.shape[1]), F32) for w in weights],
        compiler_params=_cparams(("parallel",)),
        name="mix_in",
    )(h, mods, g.reshape(1, d), *weights)


def _feat_kernel(z_ref, zp_ref, zn_ref, mu_ref, wup_ref, aup_ref, gup_ref, w0_ref, a0_ref, kk_ref, ka_ref, rk_ref,
                 ones_ref, r_out, v_out, kap_out, lw_out, kd_out, b_out, g_out, bonus_out, *, lat_tiles, d_rwkv):
    i = pl.program_id(1)
    tt = z_ref.shape[0]
    is_ctx = i == lat_tiles
    z = z_ref[...]
    first_of_seq = jnp.logical_or(i == 0, is_ctx)
    last_of_seq = jnp.logical_or(i == lat_tiles - 1, is_ctx)
    prev_row = jnp.where(first_of_seq, 0.0, zp_ref[7:8, :])
    next_row = jnp.where(last_of_seq, 0.0, zn_ref[0:1, :])
    row = lax.broadcasted_iota(jnp.int32, (tt, 1), 0)
    z_prev = jnp.where(row == 0, prev_row, pltpu.roll(z, 1, axis=0))
    z_next = jnp.where(row == tt - 1, next_row, pltpu.roll(z, tt - 1, axis=0))
    z = z + mu_ref[...] * (0.5 * (z_prev + z_next) - z)

    dr = d_rwkv
    r, k, v = z[:, :dr], z[:, dr:2 * dr], z[:, 2 * dr:3 * dr]
    low = z[:, 3 * dr:3 * dr + DECAY_RANK + ICLR_RANK]
    gd = z[:, 3 * dr + DECAY_RANK + ICLR_RANK:]
    ones = ones_ref[...]

    kap = k * kk_ref[...]
    norm = jnp.sqrt(_dot_x2(kap * kap, ones))
    kap = kap / jnp.maximum(norm, 1e-12)
    tanh_low = jnp.tanh(low)
    kd_sum = jnp.zeros_like(k)
    for dd in range(2):
        lw_out[dd, 0] = -DECAY_SCALE * _sigmoid(w0_ref[dd] + _dot_x3(tanh_low, wup_ref[dd]))
        a = _sigmoid(a0_ref[dd] + _dot_x3(low, aup_ref[dd]))
        kd = k * (1.0 + (a - 1.0) * ka_ref[...])
        kd_out[dd, 0] = kd.astype(kd_out.dtype)
        b_out[dd, 0] = (kap * a).astype(b_out.dtype)
        kd_sum = kd_sum + kd
    r_out[0] = r.astype(r_out.dtype)
    v_out[0] = v.astype(v_out.dtype)
    kap_out[0] = kap.astype(kap_out.dtype)
    g_out[0] = _dot_x3(_sigmoid(gd), gup_ref[...])
    bonus_out[0] = _dot_x2(r * kd_sum * rk_ref[...], ones) * v


def _rwkv_features(tok, za, p):
    batch, n, nc = tok.batch, tok.n_lat, tok.n_ctx
    t = n + nc
    da = za.shape[1]
    dr = p["k_k"].shape[-1]
    tt = FEAT_TILE
    lat_tiles = n // tt
    ctx_block = tok.t_lat // tt
    hb = tt // 8
    n_hblk = za.shape[0] // 8
    row = lambda name: p[name].reshape(1, dr)
    tile = lambda b, i: jnp.where(i < lat_tiles, b * lat_tiles + i, ctx_block + b)
    seq_spec = pl.BlockSpec((1, tt, dr), lambda b, i: (b, i, 0))
    dir_spec = pl.BlockSpec((2, 1, tt, dr), lambda b, i: (0, b, i, 0))
    full = lambda a: pl.BlockSpec(a.shape, lambda b, i: (0,) * a.ndim)
    consts = [p["mu"].reshape(1, da), p["w_up_pad"], p["a_up_pad"], p["g_up"],
              p["w0"].reshape(2, 1, dr), p["a0"].reshape(2, 1, dr), row("k_k"), row("k_a"), row("r_k"),
              p["head_ones"]]
    seq = lambda dt: jax.ShapeDtypeStruct((batch, t, dr), dt)
    dirs = lambda dt: jax.ShapeDtypeStruct((2, batch, t, dr), dt)
    return pl.pallas_call(
        functools.partial(_feat_kernel, lat_tiles=lat_tiles, d_rwkv=dr),
        grid=(batch, t // tt),
        in_specs=[pl.BlockSpec((tt, da), lambda b, i: (tile(b, i), 0)),
                  pl.BlockSpec((8, da), lambda b, i: (jnp.maximum(tile(b, i) * hb - 1, 0), 0)),
                  pl.BlockSpec((8, da), lambda b, i: (jnp.minimum((tile(b, i) + 1) * hb, n_hblk - 1), 0))]
                 + [full(a) for a in consts],
        out_specs=[seq_spec, seq_spec, seq_spec, dir_spec, dir_spec, dir_spec, seq_spec, seq_spec],
        out_shape=[seq(BF16), seq(BF16), seq(BF16), dirs(F32), dirs(BF16), dirs(BF16), seq(F32), seq(F32)],
        compiler_params=_cparams(("parallel", "parallel")),
        name="rwkv_features",
    )(za, za, za, *consts)


def _chunk_masks(ch, rev):
    ri = lax.broadcasted_iota(jnp.int32, (PAIR, PAIR), 0)
    ci = lax.broadcasted_iota(jnp.int32, (PAIR, PAIR), 1)
    same_head = (ri // ch) == (ci // ch)
    rt, ct = ri % ch, ci % ch
    before, upto = (ct > rt, ct >= rt) if rev else (ct < rt, ct <= rt)
    return dict(same_head=same_head, diag=ri == ci,
                strict=jnp.logical_and(same_head, before), incl=jnp.logical_and(same_head, upto))


def _scan_kernel(rf_ref, vf_ref, kapf_ref, lwf_ref, kdf_ref, bf_ref, rb_ref, vb_ref, kapb_ref, lwb_ref, kdb_ref,
                 bb_ref, yf_ref, yb_ref, s_scr):
    c = pl.program_id(1)
    ch = CHUNK
    n_sub = rf_ref.shape[1] // ch
    n_pairs = rf_ref.shape[2] // PAIR

    @pl.when(c == 0)
    def _():
        s_scr[...] = jnp.zeros_like(s_scr)

    row_id = lax.broadcasted_iota(jnp.int32, (ch, 1), 0)
    lane_head = lax.broadcasted_iota(jnp.int32, (ch, PAIR), 1) // HEAD
    head0, head1 = lane_head == 0, lane_head == 1

    def stack_heads(x):
        return jnp.concatenate([jnp.where(head0, x, 0.0), jnp.where(head1, x, 0.0)], axis=0)

    def fold_heads(x):
        return x[:ch] + x[ch:]

    chains = []
    dirs = ((rf_ref, vf_ref, kapf_ref, lwf_ref, kdf_ref, bf_ref, yf_ref),
            (rb_ref, vb_ref, kapb_ref, lwb_ref, kdb_ref, bb_ref, yb_ref))
    for d, (r_ref, v_ref, kap_ref, lw_ref, kd_ref, b_ref, y_ref) in enumerate(dirs):
        mk = _chunk_masks(ch, rev=(d == 1))
        for sub in (range(n_sub) if d == 0 else reversed(range(n_sub))):
            rows = slice(sub * ch, (sub + 1) * ch)
            lw_all = lw_ref[0, 0, rows, :]
            cl_all = lw_all
            for step in (1 << k for k in range(int(np.log2(ch)))):
                if d == 0:
                    cl_all = cl_all + jnp.where(row_id >= step, pltpu.roll(cl_all, step, axis=0), 0.0)
                else:
                    cl_all = cl_all + jnp.where(row_id < ch - step, pltpu.roll(cl_all, ch - step, axis=0), 0.0)
            for p in range(n_pairs):
                sl = slice(p * PAIR, (p + 1) * PAIR)
                f32 = lambda ref, *lead: ref[(*lead, rows, sl)].astype(F32)
                chains.append(dict(d=d, p=p, sl=sl, rows=rows, mk=mk, y_ref=y_ref, lw=lw_all[:, sl], cl=cl_all[:, sl],
                                   r=f32(r_ref, 0), v=f32(v_ref, 0), kap=f32(kap_ref, 0),
                                   kd=f32(kd_ref, 0, 0), b=f32(b_ref, 0, 0)))

    for q in chains:
        lw, cl, mk = q["lw"], q["cl"], q["mk"]
        tot = jnp.sum(lw, axis=0, keepdims=True)
        g_inv, g_rem = jnp.exp(-cl), jnp.exp(tot - cl)
        kq = stack_heads(q["kap"] * jnp.exp(cl - lw))
        rq = stack_heads(q["r"] * jnp.exp(cl))
        kdd, bdd = q["kd"] * g_inv, q["b"] * g_inv
        gram = _dot_nt(jnp.concatenate([kq, rq], axis=0).astype(BF16),
                       jnp.concatenate([bdd, bdd, kdd, kdd], axis=0).astype(BF16))
        q.update(kq=kq, rq=rq, g_tot=jnp.exp(tot),
                 kb=jnp.concatenate([q["kd"] * g_rem, q["b"] * g_rem], axis=0).astype(BF16),
                 l_b=jnp.where(mk["strict"], gram[:PAIR, :PAIR], 0.0),
                 l_k=jnp.where(mk["strict"], gram[:PAIR, PAIR:], 0.0),
                 pkb=jnp.concatenate([jnp.where(mk["incl"], gram[PAIR:, PAIR:], 0.0),
                                      jnp.where(mk["incl"], -gram[PAIR:, :PAIR], 0.0)], axis=1).astype(BF16))
    for q in chains:
        q["v2"] = jnp.concatenate([q["v"], q["v"]], axis=0).astype(BF16)
        lkv = jnp.where(q["mk"]["same_head"], _dot(q["l_k"].astype(BF16), q["v2"]), 0.0)
        q["rhs"] = jnp.concatenate([q["kq"], lkv], axis=1).astype(BF16)
        q["lb"] = q["l_b"].astype(BF16)
    for q in chains:
        q["lp"] = _dot(q["lb"], q["lb"]).astype(BF16)
        q["t"] = jnp.where(q["mk"]["diag"], 1.0, 0.0) - q["l_b"]
    n_sq = int(np.log2(ch))
    for k in range(1, n_sq):
        for q in chains:
            w = jnp.concatenate([q["t"].astype(BF16), q["lp"]], axis=1) if k < n_sq - 1 else q["t"].astype(BF16)
            q["both"] = _dot(q["lp"], w)
        for q in chains:
            q["t"] = q["t"] + q["both"][:, :PAIR]
            if k < n_sq - 1:
                q["lp"] = q["both"][:, PAIR:].astype(BF16)
    for q in chains:
        q["x"] = _dot(q["t"].astype(BF16), q["rhs"])
    for q in chains:
        x = q["x"]
        mk = q["mk"]
        xk, w1 = x[:, :PAIR], x[:, PAIR:]
        rhs = jnp.concatenate([jnp.concatenate([q["v2"], jnp.zeros_like(q["v2"])], axis=1),
                               jnp.concatenate([w1, xk], axis=1).astype(BF16)], axis=0)
        big = _dot(q["pkb"], rhs)
        q["y0"] = fold_heads(jnp.where(mk["same_head"], big[:, :PAIR], 0.0))
        q["r_eff"] = (q["rq"] + big[:, PAIR:]).astype(BF16)
        lhs = jnp.concatenate([jnp.concatenate([q["v"], -fold_heads(w1)], axis=0),
                               jnp.concatenate([jnp.zeros_like(q["v"]), -fold_heads(xk)], axis=0)], axis=1)
        upd = _dot(lhs.T.astype(BF16), q["kb"])
        q["z0"] = jnp.where(mk["same_head"], upd[:PAIR], 0.0)
        q["a"] = jnp.where(mk["diag"], q["g_tot"], jnp.where(mk["same_head"], upd[PAIR:], 0.0)).astype(BF16)
    for q in chains:
        s0 = s_scr[q["d"], q["p"]].astype(BF16)
        q["y_ref"][0, q["rows"], q["sl"]] =q["y0"] + fold_heads(_dot_nt(q["r_eff"], s0))
        s_scr[q["d"], q["p"]] = _dot(s0, q["a"]) + q["z0"]


def _rwkv_scan(r, v, kap, lw, kd, b, n_ctx):
    batch, t, dr = r.shape
    rows = SCAN_CHUNKS * CHUNK
    assert t % rows == 0 and n_ctx % rows == 0
    n_blocks, ctx_blocks = t // rows, n_ctx // rows
    lat_blocks = n_blocks - ctx_blocks

    def fwd_block(c):
        return jnp.where(c < ctx_blocks, lat_blocks + c, c - ctx_blocks)

    def bwd_block(c):
        return n_blocks - 1 - c

    fwd_seq = pl.BlockSpec((1, rows, dr), lambda bb, c: (bb, fwd_block(c), 0))
    bwd_seq = pl.BlockSpec((1, rows, dr), lambda bb, c: (bb, bwd_block(c), 0))
    fwd_dir = pl.BlockSpec((1, 1, rows, dr), lambda bb, c: (0, bb, fwd_block(c), 0))
    bwd_dir = pl.BlockSpec((1, 1, rows, dr), lambda bb, c: (1, bb, bwd_block(c), 0))
    y_shape = jax.ShapeDtypeStruct((batch, t, dr), F32)
    return pl.pallas_call(
        _scan_kernel,
        grid=(batch, n_blocks),
        in_specs=[fwd_seq, fwd_seq, fwd_seq, fwd_dir, fwd_dir, fwd_dir,
                  bwd_seq, bwd_seq, bwd_seq, bwd_dir, bwd_dir, bwd_dir],
        out_specs=[fwd_seq, bwd_seq],
        out_shape=[y_shape, y_shape],
        scratch_shapes=[pltpu.VMEM((2, dr // PAIR, PAIR, PAIR), F32)],
        compiler_params=_cparams(("parallel", "arbitrary")),
        name="rwkv_scan",
    )(r, v, kap, lw, kd, b, r, v, kap, lw, kd, b)


def _even_out_kernel(h_ref, m_ref, yf_ref, yb_ref, g_ref, bonus_ref, fl_ref, fc_ref, gnw_ref, gnb_ref, mean_ref,
                     wo_ref, wf_ref, o_ref, *, lat_tiles):
    d = h_ref.shape[1]
    y = yf_ref[0] + yb_ref[0]
    mean_mat = mean_ref[...]
    dev = y - _dot_x2(y, mean_mat)
    var = _dot_x2(dev * dev, mean_mat)
    yn = dev * lax.rsqrt(var + GN_EPS) * gnw_ref[...] + gnb_ref[...]
    o = ((yn + bonus_ref[0]) * g_ref[0]).astype(BF16)
    f = jnp.where(pl.program_id(0) < lat_tiles, fl_ref[...], fc_ref[...]).astype(BF16)
    mixed = _dot(o, wo_ref[...]) + _dot(f, wf_ref[...])
    o_ref[...] = h_ref[...] + m_ref[0, :, 5 * d:6 * d] * mixed


def _even_out(tok, h, mods, seq_arrays, f_lat, f_ctx, p):
    d, batch, n = tok.d, tok.batch, tok.n_lat
    dr = p["gn_w"].shape[-1]
    tt = FEAT_TILE
    per_batch = n // tt
    lat_tiles = batch * per_batch
    seq_spec = pl.BlockSpec((1, tt, dr), lambda t: (jnp.where(t < lat_tiles, t // per_batch, t - lat_tiles),
                                                    jnp.where(t < lat_tiles, t % per_batch, per_batch), 0))
    df = f_lat.shape[-1]
    vec = lambda a: pl.BlockSpec((1, a.shape[-1]), lambda t: (0, 0))
    return pl.pallas_call(
        functools.partial(_even_out_kernel, lat_tiles=lat_tiles),
        grid=(tok.t_all // tt,),
        in_specs=[pl.BlockSpec((tt, d), lambda t: (t, 0)),
                  pl.BlockSpec((1, 1, N_MOD * d), tok.mod_index(tt))]
                 + [seq_spec] * len(seq_arrays)
                 + [pl.BlockSpec((tt, df), lambda t: (jnp.minimum(t, lat_tiles - 1), 0)),
                    pl.BlockSpec((tt, df), lambda t: (jnp.maximum(t - lat_tiles, 0), 0)),
                    vec(p["gn_w"]), vec(p["gn_b"]), _resident(p["head_mean"].shape),
                    _resident(p["w_out_o"].shape), _resident(p["w_out_f"].shape)],
        out_specs=pl.BlockSpec((tt, d), lambda t: (t, 0)),
        out_shape=jax.ShapeDtypeStruct((tok.t_all, d), F32),
        compiler_params=_cparams(("parallel",)),
        name="even_out",
    )(h, mods, *seq_arrays, f_lat.reshape(-1, df), f_ctx.reshape(-1, df), p["gn_w"].reshape(1, dr),
      p["gn_b"].reshape(1, dr), p["head_mean"], p["w_out_o"], p["w_out_f"])


def _fourier_kernel(cs_ref, u_ref, cg_ref, sg_ref, o_ref, stacked_scr):
    n = u_ref.shape[0]

    @pl.when(pl.program_id(1) == 0)
    def _():
        u_hi, u_lo = _split2(u_ref[...])
        stacked_scr[:n] = (_dot(u_hi, cg_ref[...]) + _dot(u_lo, cg_ref[...])).astype(BF16)
        stacked_scr[n:] = (_dot(u_hi, sg_ref[...]) + _dot(u_lo, sg_ref[...])).astype(BF16)

    o_ref[0] = _dot(cs_ref[...], stacked_scr[...])


def _dft_tables(n, groups, col_major):
    n_hi = n // GRID_W
    idx = jnp.arange(n, dtype=jnp.int32)
    pos = (idx % n_hi) * GRID_W + idx // n_hi if col_major else idx
    k = pos[:, None]
    ang_hi = ((k * (jnp.arange(n_hi, dtype=jnp.int32) * GRID_W)[None, :]) % n).astype(F32) * (2.0 * np.pi / n)
    ang_lo = ((k * jnp.arange(GRID_W, dtype=jnp.int32)[None, :]) % n).astype(F32) * (2.0 * np.pi / n)
    scale = 1.0 / np.sqrt(n * FOURIER_GROUP)
    c_hi, s_hi = jnp.cos(ang_hi) * scale, jnp.sin(ang_hi) * scale
    c_lo, s_lo = jnp.cos(ang_lo), jnp.sin(ang_lo)
    if col_major:
        hi, lo = (lambda a: a[:, None, :]), (lambda a: a[:, :, None])
    else:
        hi, lo = (lambda a: a[:, :, None]), (lambda a: a[:, None, :])
    cos = (hi(c_hi) * lo(c_lo) - hi(s_hi) * lo(s_lo)).reshape(n, n)
    sin = (hi(s_hi) * lo(c_lo) + hi(c_hi) * lo(s_lo)).reshape(n, n)
    cs = jnp.concatenate([cos, -sin], axis=1).astype(BF16)
    kg = np.arange(FOURIER_GROUP)
    ang_g = 2.0 * np.pi * ((kg[:, None] * kg[None, :]) % FOURIER_GROUP) / FOURIER_GROUP
    eye = np.eye(groups)
    cg = jnp.asarray(np.kron(eye, np.cos(ang_g)), F32).astype(BF16)
    sg = jnp.asarray(np.kron(eye, np.sin(ang_g)), F32).astype(BF16)
    return cs, cg, sg


def _fourier_mix(zf, batch, n, first_block, col_major=False):
    ch = zf.shape[1]
    cs, cg, sg = _dft_tables(n, ch // FOURIER_GROUP, col_major)
    tn = min(n, 512)
    return pl.pallas_call(
        _fourier_kernel,
        grid=(batch, n // tn),
        in_specs=[pl.BlockSpec((tn, 2 * n), lambda b, i: (i, 0)),
                  pl.BlockSpec((n, ch), lambda b, i: (first_block + b, 0)),
                  pl.BlockSpec((ch, ch), lambda b, i: (0, 0)),
                  pl.BlockSpec((ch, ch), lambda b, i: (0, 0))],
        out_specs=pl.BlockSpec((1, tn, ch), lambda b, i: (b, i, 0)),
        out_shape=jax.ShapeDtypeStruct((batch, n, ch), F32),
        scratch_shapes=[pltpu.VMEM((2 * n, ch), BF16)],
        compiler_params=_cparams(("parallel", "arbitrary")),
        name="fourier_mix",
    )(cs, zf, cg, sg)


def _odd_kernel(h_ref, m_ref, z_ref, zp_ref, zn_ref, cw_ref, cb_ref, cg_ref, pw_ref, ps_ref, wu_ref, wp_ref,
                o_ref, u_scr, *, lat_tiles, lat_seq_tiles, n_lat, n_ctx, d_conv):
    i = pl.program_id(0)
    tt, d = h_ref.shape
    is_lat = i < lat_tiles
    seq_tile = jnp.where(is_lat, i % lat_seq_tiles, 0)
    seq_tiles = jnp.where(is_lat, lat_seq_tiles, n_ctx // tt)
    seq_len = jnp.where(is_lat, n_lat, n_ctx)
    zp = jnp.where(seq_tile == 0, 0.0, zp_ref[...])
    zn = jnp.where(seq_tile == seq_tiles - 1, 0.0, zn_ref[...])
    ext = jnp.concatenate([zp, z_ref[...], zn], axis=0)

    dc = d_conv
    rows = tt + 2 * HALO
    glu = ext[:, :dc] * _sigmoid(ext[:, dc:2 * dc])
    u_scr[0] = glu
    for r in range(1, 8):
        u_scr[r] = pltpu.roll(glu, rows - r, axis=0)
    acc = jnp.zeros((tt, dc), F32) + cb_ref[...]
    for k in range(CONV_WIDTH):
        shift = HALO - CONV_WIDTH // 2 + k
        acc = acc + cw_ref[k:k + 1, :] * u_scr[shift % 8, pl.ds(shift - shift % 8, tt), :]
    u = acc * lax.rsqrt(jnp.mean(acc * acc, axis=-1, keepdims=True) + NORM_EPS) * cg_ref[...]
    u = _silu(u)

    q = ext[:, 2 * dc:]
    sums, win, step = [], q, 1
    for width in POOL_WIDTHS:
        if width == 2:
            win = q + pltpu.roll(q, 1, axis=0)
        else:
            win = pltpu.roll(win, step, axis=0) + pltpu.roll(win, rows - step, axis=0)
            step *= 2
        sums.append(win[HALO:HALO + tt])
    n_groups = len(POOL_WIDTHS)
    group = lax.broadcasted_iota(jnp.int32, (tt, q.shape[1]), 1) // (q.shape[1] // n_groups)
    pos = seq_tile * tt + lax.broadcasted_iota(jnp.int32, (tt, q.shape[1]), 0)
    total = sums[-1]
    width_of = jnp.full(group.shape, POOL_WIDTHS[-1], jnp.int32)
    for gi in range(n_groups - 2, -1, -1):
        total = jnp.where(group == gi, sums[gi], total)
        width_of = jnp.where(group == gi, POOL_WIDTHS[gi], width_of)
    lo = jnp.maximum(pos - width_of // 2, 0)
    hi = jnp.minimum(pos + (width_of - 1 - width_of // 2), seq_len - 1)
    pooled = total / (hi - lo + 1).astype(F32) - q[HALO:HALO + tt]
    pooled = _dot(pooled.astype(BF16), pw_ref[...]) * ps_ref[...]

    mixed = _dot(u.astype(BF16), wu_ref[...]) + _dot(pooled.astype(BF16), wp_ref[...])
    o_ref[...] = h_ref[...] + m_ref[0, :, 5 * d:6 * d] * mixed


def _odd_mixer(tok, h, mods, z, p, *, with_ctx):
    d, d_in = tok.d, z.shape[1]
    dc = p["conv_w"].shape[1]
    dp = d_in - 2 * dc
    tt = 256
    rows = tok.rows(with_ctx)
    hb, n_hblk = tt // HALO, z.shape[0] // HALO
    full = lambda a: pl.BlockSpec(a.shape, lambda i: (0,) * a.ndim)
    consts = [p["conv_w"], p["conv_b"].reshape(1, dc), p["cnorm_g"].reshape(1, dc), p["pool_w_bd"],
              p["pool_scale"].reshape(1, dp), p["w_out_u"], p["w_out_p"]]
    return pl.pallas_call(
        functools.partial(_odd_kernel, lat_tiles=tok.t_lat // tt, lat_seq_tiles=tok.n_lat // tt,
                          n_lat=tok.n_lat, n_ctx=tok.n_ctx, d_conv=dc),
        grid=(rows // tt,),
        in_specs=[pl.BlockSpec((tt, d), lambda i: (i, 0)),
                  pl.BlockSpec((1, 1, N_MOD * d), tok.mod_index(tt)),
                  pl.BlockSpec((tt, d_in), lambda i: (i, 0)),
                  pl.BlockSpec((HALO, d_in), lambda i: (jnp.maximum(i * hb - 1, 0), 0)),
                  pl.BlockSpec((HALO, d_in), lambda i: (jnp.minimum((i + 1) * hb, n_hblk - 1), 0))]
                 + [full(a) for a in consts],
        out_specs=pl.BlockSpec((tt, d), lambda i: (i, 0)),
        out_shape=jax.ShapeDtypeStruct((rows, d), F32),
        scratch_shapes=[pltpu.VMEM((8, tt + 2 * HALO, dc), F32)],
        compiler_params=_cparams(("parallel",)),
        name="odd_mixer",
    )(h, mods, z, z, z, *consts)


def _reorder_latents(tok, h, to_col_major):
    n_rows = tok.n_lat // GRID_W
    shape = (tok.batch, n_rows, GRID_W, tok.d) if to_col_major else (tok.batch, GRID_W, n_rows, tok.d)
    lat = h[:tok.t_lat].reshape(shape).transpose(0, 2, 1, 3).reshape(tok.t_lat, tok.d)
    return jnp.concatenate([lat, h[tok.t_lat:]], axis=0)


def _even_mixer(tok, h, mods, norm_g, p, col_major):
    za, zf = _mix_in(tok, h, mods, norm_g, [p["w_in_a"], p["w_in_f"]], with_ctx=True)
    f_lat = _fourier_mix(zf, tok.batch, tok.n_lat, 0, col_major)
    f_ctx = _fourier_mix(zf, tok.batch, tok.n_ctx, tok.t_lat // tok.n_ctx)
    r, v, kap, lw, kd, bb, g, bonus = _rwkv_features(tok, za, p)
    y_fwd, y_bwd = _rwkv_scan(r, v, kap, lw, kd, bb, tok.n_ctx)
    return _even_out(tok, h, mods, [y_fwd, y_bwd, g, bonus], f_lat, f_ctx, p)


def _pad_rows(w, before, total):
    return jnp.pad(w, ((0, 0), (before, total - before - w.shape[1]), (0, 0)))


def _block_diag(blocks):
    g, a, b = blocks.shape
    eye = jnp.eye(g, dtype=blocks.dtype)
    return (eye[:, None, :, None] * blocks[:, :, None, :]).reshape(g * a, g * b)


def kernel(x, c, ctx, c_ctx, ada_w, ada_b, norm_g, ffn1_w_gu, ffn1_w_down, ffn2_w_gu, ffn2_w_down,
           e_w_in, e_mu, e_w0, e_w_up, e_a0, e_a_up, e_g_up, e_k_k, e_k_a, e_r_k, e_gn_w, e_gn_b, e_w_out,
           o_w_in, o_conv_w, o_conv_b, o_cnorm_g, o_pool_w, o_pool_scale, o_w_out, final_g):
    batch, n_lat, d = x.shape
    n_ctx = ctx.shape[1]
    depth = ada_w.shape[0]
    tok = _Tokens(batch, n_lat, n_ctx, d)
    assert batch < MOD_ROWS and n_lat % 512 == 0 and (batch * n_ctx) % 512 == 0 and n_ctx == FEAT_TILE
    assert n_lat % (GRID_W * 8) == 0 and FEAT_TILE % (n_lat // GRID_W) == 0 and FEAT_TILE % GRID_W == 0

    cc = jnp.zeros((MOD_ROWS, d), F32).at[:batch].set(c).at[batch].set(c_ctx)
    mods_all = _ada_table(cc, ada_w, ada_b).reshape(depth, MOD_ROWS, 1, N_MOD * d)
    h = jnp.concatenate([x.reshape(batch * n_lat, d), ctx.reshape(batch * n_ctx, d)], axis=0)

    dr = e_k_k.shape[-1]
    low_rank = DECAY_RANK + ICLR_RANK
    head_ones = _block_diag(jnp.ones((dr // HEAD, HEAD, HEAD), F32))
    for i in range(depth):
        j = i // 2
        with_ctx = not (i == depth - 1 and i % 2 == 1)
        mods = mods_all[i]
        h = _half_ffn(tok, h, mods, norm_g[i, 0], ffn1_w_gu[i].astype(BF16), ffn1_w_down[i].astype(BF16), final_g,
                      off=0, with_ctx=with_ctx)
        if i % 2 == 0:
            da = e_mu.shape[-1]
            p = dict(w_in_a=e_w_in[j, :, :da].astype(BF16), w_in_f=e_w_in[j, :, da:].astype(BF16),
                     mu=e_mu[j], w0=e_w0[j], a0=e_a0[j], g_up=e_g_up[j],
                     w_up_pad=_pad_rows(e_w_up[j], 0, low_rank), a_up_pad=_pad_rows(e_a_up[j], DECAY_RANK, low_rank),
                     k_k=e_k_k[j], k_a=e_k_a[j], r_k=e_r_k[j], gn_w=e_gn_w[j], gn_b=e_gn_b[j],
                     w_out_o=e_w_out[j, :dr].astype(BF16), w_out_f=e_w_out[j, dr:].astype(BF16),
                     head_ones=head_ones.astype(BF16),
                     head_mean=(head_ones / HEAD).astype(BF16))
            col_major = j % 2 == 1
            if col_major:
                h = _reorder_latents(tok, h, to_col_major=True)
            h = _even_mixer(tok, h, mods, norm_g[i, 1], p, col_major)
            if col_major:
                h = _reorder_latents(tok, h, to_col_major=False)
        else:
            dc = o_conv_w.shape[-1]
            p = dict(conv_w=o_conv_w[j], conv_b=o_conv_b[j], cnorm_g=o_cnorm_g[j],
                     pool_w_bd=_block_diag(o_pool_w[j]).astype(BF16), pool_scale=o_pool_scale[j],
                     w_out_u=o_w_out[j, :dc].astype(BF16), w_out_p=o_w_out[j, dc:].astype(BF16))
            (z,) = _mix_in(tok, h, mods, norm_g[i, 1], [o_w_in[j].astype(BF16)], with_ctx=with_ctx)
            h = _odd_mixer(tok, h, mods, z, p, with_ctx=with_ctx)
        h = _half_ffn(tok, h, mods, norm_g[i, 2], ffn2_w_gu[i].astype(BF16), ffn2_w_down[i].astype(BF16), final_g,
                      off=6, with_ctx=with_ctx, final=(i == depth - 1))
    return h[:tok.t_lat].reshape(batch, n_lat, d)
```

```python
import functools

import jax
import jax.numpy as jnp
import numpy as np
from jax import lax
from jax.experimental import pallas as pl
from jax.experimental.pallas import tpu as pltpu

F32 = jnp.float32
BF16 = jnp.bfloat16

N_MOD = 9
NORM_EPS = 1e-6
GRID_W = 64
HEAD = 64
PAIR = 2 * HEAD
DECAY_RANK = 64
ICLR_RANK = 64
GATE_RANK = 128
DECAY_SCALE = 0.606531
GN_EPS = 64e-5
KAP_NORM_EPS = 1e-12
HEAD_SUM_LANES = 256
FOURIER_GROUP = 64
CONV_WIDTH = 31
POOL_WIDTHS = (2, 4, 8, 16)
CHUNK = 64
SCAN_CHUNKS = 4
HALO = 16
FEAT_TILE = 256
MOD_ROWS = 16
VMEM_LIMIT = 56 * 1024 * 1024


def _cparams(sem):
    return pltpu.CompilerParams(dimension_semantics=sem, vmem_limit_bytes=VMEM_LIMIT)


def _dot(a, b):
    return jnp.dot(a, b, preferred_element_type=F32)


def _dot_nt(a, b):
    return lax.dot_general(a, b, (((1,), (1,)), ((), ())), preferred_element_type=F32)


def _split2(x):
    hi = x.astype(BF16)
    lo = (x - hi.astype(F32)).astype(BF16)
    return hi, lo


def _dot_x2(x, w_bf16):
    hi, lo = _split2(x)
    return _dot(hi, w_bf16) + _dot(lo, w_bf16)


def _lane_block_dot_x2(x, w_bf16):
    hi, lo = _split2(x)
    n = w_bf16.shape[0]
    return jnp.concatenate([_dot(hi[:, j:j + n], w_bf16) + _dot(lo[:, j:j + n], w_bf16)
                            for j in range(0, x.shape[1], n)], axis=1)


def _dot_x3(x, w):
    xh, xl = _split2(x)
    wh, wl = _split2(w)
    return _dot(xh, wh) + (_dot(xh, wl) + _dot(xl, wh))


def _sigmoid(x):
    return 0.5 * jnp.tanh(0.5 * x) + 0.5


def _silu(x):
    return x * _sigmoid(x)


def _modulate(h, g, shift, scale):
    y = h * lax.rsqrt(jnp.mean(h * h, axis=-1, keepdims=True) + NORM_EPS)
    return y * g * (1.0 + scale) + shift


def _ada_kernel(c_ref, w_ref, b_ref, o_ref):
    o_ref[0] = _dot_x3(_silu(c_ref[...]), w_ref[0]) + b_ref[0]


def _ada_table(cc, ada_w, ada_b):
    depth, d, nd = ada_w.shape
    tn = 1024
    return pl.pallas_call(
        _ada_kernel,
        grid=(depth, nd // tn),
        in_specs=[pl.BlockSpec((MOD_ROWS, d), lambda i, j: (0, 0)),
                  pl.BlockSpec((1, d, tn), lambda i, j: (i, 0, j)),
                  pl.BlockSpec((1, 1, tn), lambda i, j: (i, 0, j))],
        out_specs=pl.BlockSpec((1, MOD_ROWS, tn), lambda i, j: (i, 0, j)),
        out_shape=jax.ShapeDtypeStruct((depth, MOD_ROWS, nd), F32),
        compiler_params=_cparams(("parallel", "parallel")),
        name="ada_table",
    )(cc, ada_w, ada_b.reshape(depth, 1, nd))


class _Tokens:
    def __init__(self, batch, n_lat, n_ctx, d):
        self.batch, self.n_lat, self.n_ctx, self.d = batch, n_lat, n_ctx, d
        self.t_lat = batch * n_lat
        self.t_all = batch * (n_lat + n_ctx)

    def rows(self, with_ctx):
        return self.t_all if with_ctx else self.t_lat

    def mod_index(self, tm):
        lat_tiles, per_batch, ctx_row = self.t_lat // tm, self.n_lat // tm, self.batch
        return lambda t, *_: (jnp.where(t < lat_tiles, t // per_batch, ctx_row), 0, 0)


def _mod_slices(m_ref, off, d):
    return [m_ref[0, :, (off + k) * d:(off + k + 1) * d] for k in range(3)]


def _resident(shape):
    return pl.BlockSpec(shape, lambda *_: (0,) * len(shape), pipeline_mode=pl.Buffered(1))


def _ffn_kernel(h_ref, m_ref, g_ref, wgu_ref, wd_ref, fg_ref, o_ref, *, off, final):
    d, ff = h_ref.shape[1], wd_ref.shape[0]
    shift, scale, gate = _mod_slices(m_ref, off, d)
    h = h_ref[...]
    y = _modulate(h, g_ref[...], shift, scale).astype(BF16)
    gu = _dot(y, wgu_ref[...])
    act = (_silu(gu[:, :ff]) * gu[:, ff:]).astype(BF16)
    out = h + (0.5 * gate) * _dot(act, wd_ref[...])
    if final:
        out = out * lax.rsqrt(jnp.mean(out * out, axis=-1, keepdims=True) + NORM_EPS) * fg_ref[...]
    o_ref[...] = out


def _half_ffn(tok, h, mods, g, w_gu, w_down, final_g, *, off, with_ctx, final=False):
    d = tok.d
    tm = 512
    rows = tok.rows(with_ctx)
    return pl.pallas_call(
        functools.partial(_ffn_kernel, off=off, final=final),
        grid=(rows // tm,),
        in_specs=[pl.BlockSpec((tm, d), lambda t: (t, 0)),
                  pl.BlockSpec((1, 1, N_MOD * d), tok.mod_index(tm)),
                  pl.BlockSpec((1, d), lambda t: (0, 0)),
                  _resident(w_gu.shape),
                  _resident(w_down.shape),
                  pl.BlockSpec((1, d), lambda t: (0, 0))],
        out_specs=pl.BlockSpec((tm, d), lambda t: (t, 0)),
        out_shape=jax.ShapeDtypeStruct((rows, d), F32),
        compiler_params=_cparams(("parallel",)),
        name="half_ffn",
    )(h, mods, g.reshape(1, d), w_gu, w_down, final_g.reshape(1, d))


def _mix_in_kernel(h_ref, m_ref, g_ref, *refs):
    d = h_ref.shape[1]
    w_refs, z_refs = refs[:len(refs) // 2], refs[len(refs) // 2:]
    shift, scale, _ = _mod_slices(m_ref, 3, d)
    y = _modulate(h_ref[...], g_ref[...], shift, scale).astype(BF16)
    for w_ref, z_ref in zip(w_refs, z_refs):
        z_ref[...] = _dot(y, w_ref[...])


def _mix_in(tok, h, mods, g, weights, *, with_ctx):
    d = tok.d
    tm = 512
    rows = tok.rows(with_ctx)
    return pl.pallas_call(
        _mix_in_kernel,
        grid=(rows // tm,),
        in_specs=[pl.BlockSpec((tm, d), lambda t: (t, 0)),
                  pl.BlockSpec((1, 1, N_MOD * d), tok.mod_index(tm)),
                  pl.BlockSpec((1, d), lambda t: (0, 0))]
                 + [_resident(w.shape) for w in weights],
        out_specs=[pl.BlockSpec((tm, w.shape[1]), lambda t: (t, 0)) for w in weights],
        out_shape=[jax.ShapeDtypeStruct((rows, w.shape[1]), F32) for w in weights],
        compiler_params=_cparams(("parallel",)),
        name="mix_in",
    )(h, mods, g.reshape(1, d), *weights)


def _feat_kernel(z_ref, zp_ref, zn_ref, mu_ref, wup_ref, aup_ref, gup_ref, w0_ref, a0_ref, kk_ref, ka_ref, rk_ref,
                 ones_ref, r_out, v_out, kap_out, lw_out, kd_out, b_out, g_out, bonus_out, *, lat_tiles, d_rwkv):
    i = pl.program_id(1)
    tt = z_ref.shape[0]
    is_ctx = i == lat_tiles
    z = z_ref[...]
    first_of_seq = jnp.logical_or(i == 0, is_ctx)
    last_of_seq = jnp.logical_or(i == lat_tiles - 1, is_ctx)
    prev_row = jnp.where(first_of_seq, 0.0, zp_ref[7:8, :])
    next_row = jnp.where(last_of_seq, 0.0, zn_ref[0:1, :])
    row = lax.broadcasted_iota(jnp.int32, (tt, 1), 0)
    z_prev = jnp.where(row == 0, prev_row, pltpu.roll(z, 1, axis=0))
    z_next = jnp.where(row == tt - 1, next_row, pltpu.roll(z, tt - 1, axis=0))
    z = z + mu_ref[...] * (0.5 * (z_prev + z_next) - z)

    dr = d_rwkv
    r, k, v = z[:, :dr], z[:, dr:2 * dr], z[:, 2 * dr:3 * dr]
    low = z[:, 3 * dr:3 * dr + DECAY_RANK + ICLR_RANK]
    gd = z[:, 3 * dr + DECAY_RANK + ICLR_RANK:]
    ones = ones_ref[...]

    kap = k * kk_ref[...]
    kap = kap * lax.rsqrt(jnp.maximum(_lane_block_dot_x2(kap * kap, ones), KAP_NORM_EPS * KAP_NORM_EPS))
    tanh_low = jnp.tanh(low)
    low_bf16 = low.astype(BF16)
    kd_sum = jnp.zeros_like(k)
    for dd in range(2):
        lw_out[dd, 0] = -DECAY_SCALE * _sigmoid(w0_ref[dd] + _dot_x2(tanh_low, wup_ref[dd]))
        a = _sigmoid(a0_ref[dd] + _dot(low_bf16, aup_ref[dd]))
        kd = k * (1.0 + (a - 1.0) * ka_ref[...])
        kd_out[dd, 0] = kd.astype(kd_out.dtype)
        b_out[dd, 0] = (kap * a).astype(b_out.dtype)
        kd_sum = kd_sum + kd
    r_out[0] = r.astype(r_out.dtype)
    v_out[0] = v.astype(v_out.dtype)
    kap_out[0] = kap.astype(kap_out.dtype)
    g_out[0] = _dot(_sigmoid(gd).astype(BF16), gup_ref[...])
    bonus_out[0] = _lane_block_dot_x2(r * kd_sum * rk_ref[...], ones) * v


def _rwkv_features(tok, za, p):
    batch, n, nc = tok.batch, tok.n_lat, tok.n_ctx
    t = n + nc
    da = za.shape[1]
    dr = p["k_k"].shape[-1]
    tt = FEAT_TILE
    lat_tiles = n // tt
    ctx_block = tok.t_lat // tt
    hb = tt // 8
    n_hblk = za.shape[0] // 8
    row = lambda name: p[name].reshape(1, dr)
    tile = lambda b, i: jnp.where(i < lat_tiles, b * lat_tiles + i, ctx_block + b)
    seq_spec = pl.BlockSpec((1, tt, dr), lambda b, i: (b, i, 0))
    dir_spec = pl.BlockSpec((2, 1, tt, dr), lambda b, i: (0, b, i, 0))
    full = lambda a: pl.BlockSpec(a.shape, lambda b, i: (0,) * a.ndim)
    consts = [p["mu"].reshape(1, da), p["w_up_pad"], p["a_up_pad"], p["g_up"],
              p["w0"].reshape(2, 1, dr), p["a0"].reshape(2, 1, dr), row("k_k"), row("k_a"), row("r_k"),
              p["head_ones"]]
    seq = lambda dt: jax.ShapeDtypeStruct((batch, t, dr), dt)
    dirs = lambda dt: jax.ShapeDtypeStruct((2, batch, t, dr), dt)
    return pl.pallas_call(
        functools.partial(_feat_kernel, lat_tiles=lat_tiles, d_rwkv=dr),
        grid=(batch, t // tt),
        in_specs=[pl.BlockSpec((tt, da), lambda b, i: (tile(b, i), 0)),
                  pl.BlockSpec((8, da), lambda b, i: (jnp.maximum(tile(b, i) * hb - 1, 0), 0)),
                  pl.BlockSpec((8, da), lambda b, i: (jnp.minimum((tile(b, i) + 1) * hb, n_hblk - 1), 0))]
                 + [full(a) for a in consts],
        out_specs=[seq_spec, seq_spec, seq_spec, dir_spec, dir_spec, dir_spec, seq_spec, seq_spec],
        out_shape=[seq(BF16), seq(BF16), seq(BF16), dirs(F32), dirs(BF16), dirs(BF16), seq(F32), seq(F32)],
        compiler_params=_cparams(("parallel", "parallel")),
        name="rwkv_features",
    )(za, za, za, *consts)


def _chunk_masks(ch, rev):
    ri = lax.broadcasted_iota(jnp.int32, (PAIR, PAIR), 0)
    ci = lax.broadcasted_iota(jnp.int32, (PAIR, PAIR), 1)
    same_head = (ri // ch) == (ci // ch)
    rt, ct = ri % ch, ci % ch
    before, upto = (ct > rt, ct >= rt) if rev else (ct < rt, ct <= rt)
    return dict(same_head=same_head, diag=ri == ci,
                strict=jnp.logical_and(same_head, before), incl=jnp.logical_and(same_head, upto))


def _scan_kernel(rf_ref, vf_ref, kapf_ref, lwf_ref, kdf_ref, bf_ref, rb_ref, vb_ref, kapb_ref, lwb_ref, kdb_ref,
                 bb_ref, yf_ref, yb_ref, s_scr):
    c = pl.program_id(1)
    ch = CHUNK
    n_sub = rf_ref.shape[1] // ch
    n_pairs = rf_ref.shape[2] // PAIR

    @pl.when(c == 0)
    def _():
        s_scr[...] = jnp.zeros_like(s_scr)

    row_id = lax.broadcasted_iota(jnp.int32, (ch, 1), 0)
    lane_head = lax.broadcasted_iota(jnp.int32, (ch, PAIR), 1) // HEAD
    head0, head1 = lane_head == 0, lane_head == 1

    def stack_heads(x):
        return jnp.concatenate([jnp.where(head0, x, 0.0), jnp.where(head1, x, 0.0)], axis=0)

    def fold_heads(x):
        return x[:ch] + x[ch:]

    chains = []
    dirs = ((rf_ref, vf_ref, kapf_ref, lwf_ref, kdf_ref, bf_ref, yf_ref),
            (rb_ref, vb_ref, kapb_ref, lwb_ref, kdb_ref, bb_ref, yb_ref))
    for d, (r_ref, v_ref, kap_ref, lw_ref, kd_ref, b_ref, y_ref) in enumerate(dirs):
        mk = _chunk_masks(ch, rev=(d == 1))
        for sub in (range(n_sub) if d == 0 else reversed(range(n_sub))):
            rows = slice(sub * ch, (sub + 1) * ch)
            lw_all = lw_ref[0, 0, rows, :]
            cl_all = lw_all
            for step in (1 << k for k in range(int(np.log2(ch)))):
                if d == 0:
                    cl_all = cl_all + jnp.where(row_id >= step, pltpu.roll(cl_all, step, axis=0), 0.0)
                else:
                    cl_all = cl_all + jnp.where(row_id < ch - step, pltpu.roll(cl_all, ch - step, axis=0), 0.0)
            for p in range(n_pairs):
                sl = slice(p * PAIR, (p + 1) * PAIR)
                f32 = lambda ref, *lead: ref[(*lead, rows, sl)].astype(F32)
                chains.append(dict(d=d, p=p, sl=sl, rows=rows, mk=mk, y_ref=y_ref, lw=lw_all[:, sl], cl=cl_all[:, sl],
                                   r=f32(r_ref, 0), v=f32(v_ref, 0), kap=f32(kap_ref, 0),
                                   kd=f32(kd_ref, 0, 0), b=f32(b_ref, 0, 0)))

    for q in chains:
        lw, cl, mk = q["lw"], q["cl"], q["mk"]
        tot = jnp.sum(lw, axis=0, keepdims=True)
        g_inv, g_rem = jnp.exp(-cl), jnp.exp(tot - cl)
        kq = stack_heads(q["kap"] * jnp.exp(cl - lw))
        rq = stack_heads(q["r"] * jnp.exp(cl))
        kdd, bdd = q["kd"] * g_inv, q["b"] * g_inv
        gram = _dot_nt(jnp.concatenate([kq, rq], axis=0).astype(BF16),
                       jnp.concatenate([bdd, bdd, kdd, kdd], axis=0).astype(BF16))
        q.update(kq=kq, rq=rq, g_tot=jnp.exp(tot),
                 kb=jnp.concatenate([q["kd"] * g_rem, q["b"] * g_rem], axis=0).astype(BF16),
                 l_b=jnp.where(mk["strict"], gram[:PAIR, :PAIR], 0.0),
                 l_k=jnp.where(mk["strict"], gram[:PAIR, PAIR:], 0.0),
                 pkb=jnp.concatenate([jnp.where(mk["incl"], gram[PAIR:, PAIR:], 0.0),
                                      jnp.where(mk["incl"], -gram[PAIR:, :PAIR], 0.0)], axis=1).astype(BF16))
    for q in chains:
        q["v2"] = jnp.concatenate([q["v"], q["v"]], axis=0).astype(BF16)
        lkv = jnp.where(q["mk"]["same_head"], _dot(q["l_k"].astype(BF16), q["v2"]), 0.0)
        q["rhs"] = jnp.concatenate([q["kq"], lkv], axis=1).astype(BF16)
        q["lb"] = q["l_b"].astype(BF16)
    for q in chains:
        q["lp"] = _dot(q["lb"], q["lb"]).astype(BF16)
        q["t"] = jnp.where(q["mk"]["diag"], 1.0, 0.0) - q["l_b"]
    n_sq = int(np.log2(ch))
    for k in range(1, n_sq):
        for q in chains:
            w = jnp.concatenate([q["t"].astype(BF16), q["lp"]], axis=1) if k < n_sq - 1 else q["t"].astype(BF16)
            q["both"] = _dot(q["lp"], w)
        for q in chains:
            q["t"] = q["t"] + q["both"][:, :PAIR]
            if k < n_sq - 1:
                q["lp"] = q["both"][:, PAIR:].astype(BF16)
    for q in chains:
        q["x"] = _dot(q["t"].astype(BF16), q["rhs"])
    for q in chains:
        x = q["x"]
        mk = q["mk"]
        xk, w1 = x[:, :PAIR], x[:, PAIR:]
        rhs = jnp.concatenate([jnp.concatenate([q["v2"], jnp.zeros_like(q["v2"])], axis=1),
                               jnp.concatenate([w1, xk], axis=1).astype(BF16)], axis=0)
        big = _dot(q["pkb"], rhs)
        q["y0"] = fold_heads(jnp.where(mk["same_head"], big[:, :PAIR], 0.0))
        q["r_eff"] = (q["rq"] + big[:, PAIR:]).astype(BF16)
        lhs = jnp.concatenate([jnp.concatenate([q["v"], -fold_heads(w1)], axis=0),
                               jnp.concatenate([jnp.zeros_like(q["v"]), -fold_heads(xk)], axis=0)], axis=1)
        upd = _dot(lhs.T.astype(BF16), q["kb"])
        q["z0"] = jnp.where(mk["same_head"], upd[:PAIR], 0.0)
        q["a"] = jnp.where(mk["diag"], q["g_tot"], jnp.where(mk["same_head"], upd[PAIR:], 0.0)).astype(BF16)
    for q in chains:
        s0 = s_scr[q["d"], q["p"]].astype(BF16)
        q["y_ref"][0, q["rows"], q["sl"]] =q["y0"] + fold_heads(_dot_nt(q["r_eff"], s0))
        s_scr[q["d"], q["p"]] = _dot(s0, q["a"]) + q["z0"]


def _rwkv_scan(r, v, kap, lw, kd, b, n_ctx):
    batch, t, dr = r.shape
    rows = SCAN_CHUNKS * CHUNK
    assert t % rows == 0 and n_ctx % rows == 0
    n_blocks, ctx_blocks = t // rows, n_ctx // rows
    lat_blocks = n_blocks - ctx_blocks

    def fwd_block(c):
        return jnp.where(c < ctx_blocks, lat_blocks + c, c - ctx_blocks)

    def bwd_block(c):
        return n_blocks - 1 - c

    fwd_seq = pl.BlockSpec((1, rows, dr), lambda bb, c: (bb, fwd_block(c), 0))
    bwd_seq = pl.BlockSpec((1, rows, dr), lambda bb, c: (bb, bwd_block(c), 0))
    fwd_dir = pl.BlockSpec((1, 1, rows, dr), lambda bb, c: (0, bb, fwd_block(c), 0))
    bwd_dir = pl.BlockSpec((1, 1, rows, dr), lambda bb, c: (1, bb, bwd_block(c), 0))
    y_shape = jax.ShapeDtypeStruct((batch, t, dr), F32)
    return pl.pallas_call(
        _scan_kernel,
        grid=(batch, n_blocks),
        in_specs=[fwd_seq, fwd_seq, fwd_seq, fwd_dir, fwd_dir, fwd_dir,
                  bwd_seq, bwd_seq, bwd_seq, bwd_dir, bwd_dir, bwd_dir],
        out_specs=[fwd_seq, bwd_seq],
        out_shape=[y_shape, y_shape],
        scratch_shapes=[pltpu.VMEM((2, dr // PAIR, PAIR, PAIR), F32)],
        compiler_params=_cparams(("parallel", "arbitrary")),
        name="rwkv_scan",
    )(r, v, kap, lw, kd, b, r, v, kap, lw, kd, b)


def _even_out_kernel(h_ref, m_ref, yf_ref, yb_ref, g_ref, bonus_ref, fl_ref, fc_ref, gnw_ref, gnb_ref, mean_ref,
                     wo_ref, wf_ref, o_ref, *, lat_tiles):
    d = h_ref.shape[1]
    y = yf_ref[0] + yb_ref[0]
    mean_mat = mean_ref[...]
    dev = y - _lane_block_dot_x2(y, mean_mat)
    var = _lane_block_dot_x2(dev * dev, mean_mat)
    yn = dev * lax.rsqrt(var + GN_EPS) * gnw_ref[...] + gnb_ref[...]
    o = ((yn + bonus_ref[0]) * g_ref[0]).astype(BF16)
    f = jnp.where(pl.program_id(0) < lat_tiles, fl_ref[...], fc_ref[...]).astype(BF16)
    mixed = _dot(o, wo_ref[...]) + _dot(f, wf_ref[...])
    o_ref[...] = h_ref[...] + m_ref[0, :, 5 * d:6 * d] * mixed


def _even_out(tok, h, mods, seq_arrays, f_lat, f_ctx, p):
    d, batch, n = tok.d, tok.batch, tok.n_lat
    dr = p["gn_w"].shape[-1]
    tt = FEAT_TILE
    per_batch = n // tt
    lat_tiles = batch * per_batch
    seq_spec = pl.BlockSpec((1, tt, dr), lambda t: (jnp.where(t < lat_tiles, t // per_batch, t - lat_tiles),
                                                    jnp.where(t < lat_tiles, t % per_batch, per_batch), 0))
    df = f_lat.shape[-1]
    vec = lambda a: pl.BlockSpec((1, a.shape[-1]), lambda t: (0, 0))
    return pl.pallas_call(
        functools.partial(_even_out_kernel, lat_tiles=lat_tiles),
        grid=(tok.t_all // tt,),
        in_specs=[pl.BlockSpec((tt, d), lambda t: (t, 0)),
                  pl.BlockSpec((1, 1, N_MOD * d), tok.mod_index(tt))]
                 + [seq_spec] * len(seq_arrays)
                 + [pl.BlockSpec((tt, df), lambda t: (jnp.minimum(t, lat_tiles - 1), 0)),
                    pl.BlockSpec((tt, df), lambda t: (jnp.maximum(t - lat_tiles, 0), 0)),
                    vec(p["gn_w"]), vec(p["gn_b"]), _resident(p["head_mean"].shape),
                    _resident(p["w_out_o"].shape), _resident(p["w_out_f"].shape)],
        out_specs=pl.BlockSpec((tt, d), lambda t: (t, 0)),
        out_shape=jax.ShapeDtypeStruct((tok.t_all, d), F32),
        compiler_params=_cparams(("parallel",)),
        name="even_out",
    )(h, mods, *seq_arrays, f_lat.reshape(-1, df), f_ctx.reshape(-1, df), p["gn_w"].reshape(1, dr),
      p["gn_b"].reshape(1, dr), p["head_mean"], p["w_out_o"], p["w_out_f"])


def _fourier_kernel(cs_ref, u_ref, cg_ref, sg_ref, o_ref, stacked_scr):
    n = u_ref.shape[0]

    @pl.when(pl.program_id(1) == 0)
    def _():
        u_hi, u_lo = _split2(u_ref[...])
        stacked_scr[:n] = (_dot(u_hi, cg_ref[...]) + _dot(u_lo, cg_ref[...])).astype(BF16)
        stacked_scr[n:] = (_dot(u_hi, sg_ref[...]) + _dot(u_lo, sg_ref[...])).astype(BF16)

    o_ref[0] = _dot(cs_ref[...], stacked_scr[...])


def _dft_tables(n, groups, col_major):
    n_hi = n // GRID_W
    idx = jnp.arange(n, dtype=jnp.int32)
    pos = (idx % n_hi) * GRID_W + idx // n_hi if col_major else idx
    k = pos[:, None]
    ang_hi = ((k * (jnp.arange(n_hi, dtype=jnp.int32) * GRID_W)[None, :]) % n).astype(F32) * (2.0 * np.pi / n)
    ang_lo = ((k * jnp.arange(GRID_W, dtype=jnp.int32)[None, :]) % n).astype(F32) * (2.0 * np.pi / n)
    scale = 1.0 / np.sqrt(n * FOURIER_GROUP)
    c_hi, s_hi = jnp.cos(ang_hi) * scale, jnp.sin(ang_hi) * scale
    c_lo, s_lo = jnp.cos(ang_lo), jnp.sin(ang_lo)
    if col_major:
        hi, lo = (lambda a: a[:, None, :]), (lambda a: a[:, :, None])
    else:
        hi, lo = (lambda a: a[:, :, None]), (lambda a: a[:, None, :])
    cos = (hi(c_hi) * lo(c_lo) - hi(s_hi) * lo(s_lo)).reshape(n, n)
    sin = (hi(s_hi) * lo(c_lo) + hi(c_hi) * lo(s_lo)).reshape(n, n)
    cs = jnp.concatenate([cos, -sin], axis=1).astype(BF16)
    kg = np.arange(FOURIER_GROUP)
    ang_g = 2.0 * np.pi * ((kg[:, None] * kg[None, :]) % FOURIER_GROUP) / FOURIER_GROUP
    eye = np.eye(groups)
    cg = jnp.asarray(np.kron(eye, np.cos(ang_g)), F32).astype(BF16)
    sg = jnp.asarray(np.kron(eye, np.sin(ang_g)), F32).astype(BF16)
    return cs, cg, sg


def _fourier_mix(zf, batch, n, first_block, col_major=False):
    ch = zf.shape[1]
    cs, cg, sg = _dft_tables(n, ch // FOURIER_GROUP, col_major)
    tn = min(n, 512)
    return pl.pallas_call(
        _fourier_kernel,
        grid=(batch, n // tn),
        in_specs=[pl.BlockSpec((tn, 2 * n), lambda b, i: (i, 0)),
                  pl.BlockSpec((n, ch), lambda b, i: (first_block + b, 0)),
                  pl.BlockSpec((ch, ch), lambda b, i: (0, 0)),
                  pl.BlockSpec((ch, ch), lambda b, i: (0, 0))],
        out_specs=pl.BlockSpec((1, tn, ch), lambda b, i: (b, i, 0)),
        out_shape=jax.ShapeDtypeStruct((batch, n, ch), F32),
        scratch_shapes=[pltpu.VMEM((2 * n, ch), BF16)],
        compiler_params=_cparams(("parallel", "arbitrary")),
        name="fourier_mix",
    )(cs, zf, cg, sg)


def _odd_kernel(h_ref, m_ref, z_ref, zp_ref, zn_ref, cw_ref, cb_ref, cg_ref, pw_ref, ps_ref, wu_ref, wp_ref,
                o_ref, u_scr, *, lat_tiles, lat_seq_tiles, n_lat, n_ctx, d_conv):
    i = pl.program_id(0)
    tt, d = h_ref.shape
    is_lat = i < lat_tiles
    seq_tile = jnp.where(is_lat, i % lat_seq_tiles, 0)
    seq_tiles = jnp.where(is_lat, lat_seq_tiles, n_ctx // tt)
    seq_len = jnp.where(is_lat, n_lat, n_ctx)
    zp = jnp.where(seq_tile == 0, 0.0, zp_ref[...])
    zn = jnp.where(seq_tile == seq_tiles - 1, 0.0, zn_ref[...])
    ext = jnp.concatenate([zp, z_ref[...], zn], axis=0)

    dc = d_conv
    rows = tt + 2 * HALO
    glu = ext[:, :dc] * _sigmoid(ext[:, dc:2 * dc])
    u_scr[0] = glu
    for r in range(1, 8):
        u_scr[r] = pltpu.roll(glu, rows - r, axis=0)
    acc = jnp.zeros((tt, dc), F32) + cb_ref[...]
    for k in range(CONV_WIDTH):
        shift = HALO - CONV_WIDTH // 2 + k
        acc = acc + cw_ref[k:k + 1, :] * u_scr[shift % 8, pl.ds(shift - shift % 8, tt), :]
    u = acc * lax.rsqrt(jnp.mean(acc * acc, axis=-1, keepdims=True) + NORM_EPS) * cg_ref[...]
    u = _silu(u)

    q = ext[:, 2 * dc:]
    sums, win, step = [], q, 1
    for width in POOL_WIDTHS:
        if width == 2:
            win = q + pltpu.roll(q, 1, axis=0)
        else:
            win = pltpu.roll(win, step, axis=0) + pltpu.roll(win, rows - step, axis=0)
            step *= 2
        sums.append(win[HALO:HALO + tt])
    n_groups = len(POOL_WIDTHS)
    group = lax.broadcasted_iota(jnp.int32, (tt, q.shape[1]), 1) // (q.shape[1] // n_groups)
    pos = seq_tile * tt + lax.broadcasted_iota(jnp.int32, (tt, q.shape[1]), 0)
    total = sums[-1]
    width_of = jnp.full(group.shape, POOL_WIDTHS[-1], jnp.int32)
    for gi in range(n_groups - 2, -1, -1):
        total = jnp.where(group == gi, sums[gi], total)
        width_of = jnp.where(group == gi, POOL_WIDTHS[gi], width_of)
    lo = jnp.maximum(pos - width_of // 2, 0)
    hi = jnp.minimum(pos + (width_of - 1 - width_of // 2), seq_len - 1)
    pooled = total / (hi - lo + 1).astype(F32) - q[HALO:HALO + tt]
    pooled = _dot(pooled.astype(BF16), pw_ref[...]) * ps_ref[...]

    mixed = _dot(u.astype(BF16), wu_ref[...]) + _dot(pooled.astype(BF16), wp_ref[...])
    o_ref[...] = h_ref[...] + m_ref[0, :, 5 * d:6 * d] * mixed


def _odd_mixer(tok, h, mods, z, p, *, with_ctx):
    d, d_in = tok.d, z.shape[1]
    dc = p["conv_w"].shape[1]
    dp = d_in - 2 * dc
    tt = 256
    rows = tok.rows(with_ctx)
    hb, n_hblk = tt // HALO, z.shape[0] // HALO
    full = lambda a: pl.BlockSpec(a.shape, lambda i: (0,) * a.ndim)
    consts = [p["conv_w"], p["conv_b"].reshape(1, dc), p["cnorm_g"].reshape(1, dc), p["pool_w_bd"],
              p["pool_scale"].reshape(1, dp), p["w_out_u"], p["w_out_p"]]
    return pl.pallas_call(
        functools.partial(_odd_kernel, lat_tiles=tok.t_lat // tt, lat_seq_tiles=tok.n_lat // tt,
                          n_lat=tok.n_lat, n_ctx=tok.n_ctx, d_conv=dc),
        grid=(rows // tt,),
        in_specs=[pl.BlockSpec((tt, d), lambda i: (i, 0)),
                  pl.BlockSpec((1, 1, N_MOD * d), tok.mod_index(tt)),
                  pl.BlockSpec((tt, d_in), lambda i: (i, 0)),
                  pl.BlockSpec((HALO, d_in), lambda i: (jnp.maximum(i * hb - 1, 0), 0)),
                  pl.BlockSpec((HALO, d_in), lambda i: (jnp.minimum((i + 1) * hb, n_hblk - 1), 0))]
                 + [full(a) for a in consts],
        out_specs=pl.BlockSpec((tt, d), lambda i: (i, 0)),
        out_shape=jax.ShapeDtypeStruct((rows, d), F32),
        scratch_shapes=[pltpu.VMEM((8, tt + 2 * HALO, dc), F32)],
        compiler_params=_cparams(("parallel",)),
        name="odd_mixer",
    )(h, mods, z, z, z, *consts)


def _reorder_latents(tok, h, to_col_major):
    n_rows = tok.n_lat // GRID_W
    shape = (tok.batch, n_rows, GRID_W, tok.d) if to_col_major else (tok.batch, GRID_W, n_rows, tok.d)
    lat = h[:tok.t_lat].reshape(shape).transpose(0, 2, 1, 3).reshape(tok.t_lat, tok.d)
    return lax.dynamic_update_slice(h, lat, (0, 0))


def _even_mixer(tok, h, mods, norm_g, p, col_major):
    za, zf = _mix_in(tok, h, mods, norm_g, [p["w_in_a"], p["w_in_f"]], with_ctx=True)
    f_lat = _fourier_mix(zf, tok.batch, tok.n_lat, 0, col_major)
    f_ctx = _fourier_mix(zf, tok.batch, tok.n_ctx, tok.t_lat // tok.n_ctx)
    r, v, kap, lw, kd, bb, g, bonus = _rwkv_features(tok, za, p)
    y_fwd, y_bwd = _rwkv_scan(r, v, kap, lw, kd, bb, tok.n_ctx)
    return _even_out(tok, h, mods, [y_fwd, y_bwd, g, bonus], f_lat, f_ctx, p)


def _pad_rows(w, before, total):
    return jnp.pad(w, ((0, 0), (before, total - before - w.shape[1]), (0, 0)))


def _block_diag(blocks):
    g, a, b = blocks.shape
    eye = jnp.eye(g, dtype=blocks.dtype)
    return (eye[:, None, :, None] * blocks[:, :, None, :]).reshape(g * a, g * b)


def kernel(x, c, ctx, c_ctx, ada_w, ada_b, norm_g, ffn1_w_gu, ffn1_w_down, ffn2_w_gu, ffn2_w_down,
           e_w_in, e_mu, e_w0, e_w_up, e_a0, e_a_up, e_g_up, e_k_k, e_k_a, e_r_k, e_gn_w, e_gn_b, e_w_out,
           o_w_in, o_conv_w, o_conv_b, o_cnorm_g, o_pool_w, o_pool_scale, o_w_out, final_g):
    batch, n_lat, d = x.shape
    n_ctx = ctx.shape[1]
    depth = ada_w.shape[0]
    tok = _Tokens(batch, n_lat, n_ctx, d)
    assert batch < MOD_ROWS and n_lat % 512 == 0 and (batch * n_ctx) % 512 == 0 and n_ctx == FEAT_TILE
    assert n_lat % (GRID_W * 8) == 0 and FEAT_TILE % (n_lat // GRID_W) == 0 and FEAT_TILE % GRID_W == 0

    cc = jnp.zeros((MOD_ROWS, d), F32).at[:batch].set(c).at[batch].set(c_ctx)
    mods_all = _ada_table(cc, ada_w, ada_b).reshape(depth, MOD_ROWS, 1, N_MOD * d)
    h = jnp.concatenate([x.reshape(batch * n_lat, d), ctx.reshape(batch * n_ctx, d)], axis=0)

    dr = e_k_k.shape[-1]
    low_rank = DECAY_RANK + ICLR_RANK
    head_ones = _block_diag(jnp.ones((HEAD_SUM_LANES // HEAD, HEAD, HEAD), F32))
    for i in range(depth):
        j = i // 2
        with_ctx = not (i == depth - 1 and i % 2 == 1)
        mods = mods_all[i]
        h = _half_ffn(tok, h, mods, norm_g[i, 0], ffn1_w_gu[i].astype(BF16), ffn1_w_down[i].astype(BF16), final_g,
                      off=0, with_ctx=with_ctx)
        if i % 2 == 0:
            da = e_mu.shape[-1]
            p = dict(w_in_a=e_w_in[j, :, :da].astype(BF16), w_in_f=e_w_in[j, :, da:].astype(BF16),
                     mu=e_mu[j], w0=e_w0[j], a0=e_a0[j], g_up=e_g_up[j].astype(BF16),
                     w_up_pad=_pad_rows(e_w_up[j], 0, low_rank).astype(BF16),
                     a_up_pad=_pad_rows(e_a_up[j], DECAY_RANK, low_rank).astype(BF16),
                     k_k=e_k_k[j], k_a=e_k_a[j], r_k=e_r_k[j], gn_w=e_gn_w[j], gn_b=e_gn_b[j],
                     w_out_o=e_w_out[j, :dr].astype(BF16), w_out_f=e_w_out[j, dr:].astype(BF16),
                     head_ones=head_ones.astype(BF16),
                     head_mean=(head_ones / HEAD).astype(BF16))
            col_major = j % 2 == 1
            if col_major:
                h = _reorder_latents(tok, h, to_col_major=True)
            h = _even_mixer(tok, h, mods, norm_g[i, 1], p, col_major)
            if col_major:
                h = _reorder_latents(tok, h, to_col_major=False)
        else:
            dc = o_conv_w.shape[-1]
            p = dict(conv_w=o_conv_w[j], conv_b=o_conv_b[j], cnorm_g=o_cnorm_g[j],
                     pool_w_bd=_block_diag(o_pool_w[j]).astype(BF16), pool_scale=o_pool_scale[j],
                     w_out_u=o_w_out[j, :dc].astype(BF16), w_out_p=o_w_out[j, dc:].astype(BF16))
            (z,) = _mix_in(tok, h, mods, norm_g[i, 1], [o_w_in[j].astype(BF16)], with_ctx=with_ctx)
            h = _odd_mixer(tok, h, mods, z, p, with_ctx=with_ctx)
        h = _half_ffn(tok, h, mods, norm_g[i, 2], ffn2_w_gu[i].astype(BF16), ffn2_w_down[i].astype(BF16), final_g,
                      off=6, with_ctx=with_ctx, final=(i == depth - 1))
    return h[:tok.t_lat].reshape(batch, n_lat, d)
```

```python
import functools

import jax
import jax.numpy as jnp
import numpy as np
from jax import lax
from jax.experimental import pallas as pl
from jax.experimental.pallas import tpu as pltpu

F32 = jnp.float32
BF16 = jnp.bfloat16

N_MOD = 9
NORM_EPS = 1e-6
GRID_W = 64
HEAD = 64
PAIR = 2 * HEAD
DECAY_RANK = 64
ICLR_RANK = 64
GATE_RANK = 128
DECAY_SCALE = 0.606531
GN_EPS = 64e-5
KAP_NORM_EPS = 1e-12
HEAD_SUM_LANES = 256
FOURIER_GROUP = 64
CONV_WIDTH = 31
POOL_WIDTHS = (2, 4, 8, 16)
CHUNK = 64
SCAN_CHUNKS = 4
HALO = 16
FEAT_TILE = 256
MOD_ROWS = 16
VMEM_LIMIT = 56 * 1024 * 1024


def _cparams(sem):
    return pltpu.CompilerParams(dimension_semantics=sem, vmem_limit_bytes=VMEM_LIMIT)


def _dot(a, b):
    return jnp.dot(a, b, preferred_element_type=F32)


def _dot_nt(a, b):
    return lax.dot_general(a, b, (((1,), (1,)), ((), ())), preferred_element_type=F32)


def _split2(x):
    hi = x.astype(BF16)
    lo = (x - hi.astype(F32)).astype(BF16)
    return hi, lo


def _dot_x2(x, w_bf16):
    hi, lo = _split2(x)
    return _dot(hi, w_bf16) + _dot(lo, w_bf16)


def _lane_block_dot_x2(x, w_bf16):
    hi, lo = _split2(x)
    n = w_bf16.shape[0]
    return jnp.concatenate([_dot(hi[:, j:j + n], w_bf16) + _dot(lo[:, j:j + n], w_bf16)
                            for j in range(0, x.shape[1], n)], axis=1)


def _dot_x3(x, w):
    xh, xl = _split2(x)
    wh, wl = _split2(w)
    return _dot(xh, wh) + (_dot(xh, wl) + _dot(xl, wh))


def _sigmoid(x):
    return 0.5 * jnp.tanh(0.5 * x) + 0.5


def _silu(x):
    return x * _sigmoid(x)


def _modulate(h, g, shift, scale):
    y = h * lax.rsqrt(jnp.mean(h * h, axis=-1, keepdims=True) + NORM_EPS)
    return y * g * (1.0 + scale) + shift


def _ada_kernel(c_ref, w_ref, b_ref, o_ref):
    o_ref[0] = _dot_x3(_silu(c_ref[...]), w_ref[0]) + b_ref[0]


def _ada_table(cc, ada_w, ada_b):
    depth, d, nd = ada_w.shape
    tn = 1024
    return pl.pallas_call(
        _ada_kernel,
        grid=(depth, nd // tn),
        in_specs=[pl.BlockSpec((MOD_ROWS, d), lambda i, j: (0, 0)),
                  pl.BlockSpec((1, d, tn), lambda i, j: (i, 0, j)),
                  pl.BlockSpec((1, 1, tn), lambda i, j: (i, 0, j))],
        out_specs=pl.BlockSpec((1, MOD_ROWS, tn), lambda i, j: (i, 0, j)),
        out_shape=jax.ShapeDtypeStruct((depth, MOD_ROWS, nd), F32),
        compiler_params=_cparams(("parallel", "parallel")),
        name="ada_table",
    )(cc, ada_w, ada_b.reshape(depth, 1, nd))


class _Tokens:
    def __init__(self, batch, n_lat, n_ctx, d):
        self.batch, self.n_lat, self.n_ctx, self.d = batch, n_lat, n_ctx, d
        self.t_lat = batch * n_lat
        self.t_all = batch * (n_lat + n_ctx)

    def rows(self, with_ctx):
        return self.t_all if with_ctx else self.t_lat

    def mod_index(self, tm):
        lat_tiles, per_batch, ctx_row = self.t_lat // tm, self.n_lat // tm, self.batch
        return lambda t, *_: (jnp.where(t < lat_tiles, t // per_batch, ctx_row), 0, 0)


def _mod_slices(m_ref, off, d):
    return [m_ref[0, :, (off + k) * d:(off + k + 1) * d] for k in range(3)]


def _resident(shape):
    return pl.BlockSpec(shape, lambda *_: (0,) * len(shape), pipeline_mode=pl.Buffered(1))


def _ffn_kernel(*refs, off, final, lat_tiles):
    hv_ref = refs[0] if lat_tiles is not None else None
    h_ref, m_ref, g_ref, wgu_ref, wd_ref, fg_ref, o_ref = refs[-7:]
    d, ff = h_ref.shape[1], wd_ref.shape[0]
    shift, scale, gate = _mod_slices(m_ref, off, d)
    h = h_ref[...]
    if hv_ref is not None:
        regrouped = jnp.concatenate([hv_ref[:, c, :] for c in range(hv_ref.shape[1])], axis=0)
        h = jnp.where(pl.program_id(0) < lat_tiles, regrouped, h)
    y = _modulate(h, g_ref[...], shift, scale).astype(BF16)
    gu = _dot(y, wgu_ref[...])
    act = (_silu(gu[:, :ff]) * gu[:, ff:]).astype(BF16)
    out = h + (0.5 * gate) * _dot(act, wd_ref[...])
    if final:
        out = out * lax.rsqrt(jnp.mean(out * out, axis=-1, keepdims=True) + NORM_EPS) * fg_ref[...]
    o_ref[...] = out


def _half_ffn(tok, h, mods, g, w_gu, w_down, final_g, *, off, with_ctx, final=False, regroup=None):
    d = tok.d
    tm = 512
    rows = tok.rows(with_ctx)
    lat_tiles = tok.t_lat // tm
    ins, specs = [], []
    if regroup is not None:
        groups = tm // (tok.n_lat // regroup)
        per_batch = regroup // groups
        ins.append(h.reshape(h.shape[0] // regroup, regroup, d))
        specs.append(pl.BlockSpec((tm // groups, groups, d),
                                  lambda t: (jnp.minimum(t, lat_tiles - 1) // per_batch,
                                             jnp.minimum(t, lat_tiles - 1) % per_batch, 0)))
        h_spec = pl.BlockSpec((tm, d), lambda t: (jnp.maximum(t, lat_tiles), 0))
    else:
        h_spec = pl.BlockSpec((tm, d), lambda t: (t, 0))
    return pl.pallas_call(
        functools.partial(_ffn_kernel, off=off, final=final, lat_tiles=lat_tiles if regroup is not None else None),
        grid=(rows // tm,),
        in_specs=specs + [h_spec,
                          pl.BlockSpec((1, 1, N_MOD * d), tok.mod_index(tm)),
                          pl.BlockSpec((1, d), lambda t: (0, 0)),
                          _resident(w_gu.shape),
                          _resident(w_down.shape),
                          pl.BlockSpec((1, d), lambda t: (0, 0))],
        out_specs=pl.BlockSpec((tm, d), lambda t: (t, 0)),
        out_shape=jax.ShapeDtypeStruct((rows, d), F32),
        compiler_params=_cparams(("parallel",)),
        name="half_ffn",
    )(*ins, h, mods, g.reshape(1, d), w_gu, w_down, final_g.reshape(1, d))


def _mix_in_kernel(h_ref, m_ref, g_ref, *refs):
    d = h_ref.shape[1]
    w_refs, z_refs = refs[:len(refs) // 2], refs[len(refs) // 2:]
    shift, scale, _ = _mod_slices(m_ref, 3, d)
    y = _modulate(h_ref[...], g_ref[...], shift, scale).astype(BF16)
    for w_ref, z_ref in zip(w_refs, z_refs):
        z_ref[...] = _dot(y, w_ref[...])


def _mix_in(tok, h, mods, g, weights, *, with_ctx):
    d = tok.d
    tm = 512
    rows = tok.rows(with_ctx)
    return pl.pallas_call(
        _mix_in_kernel,
        grid=(rows // tm,),
        in_specs=[pl.BlockSpec((tm, d), lambda t: (t, 0)),
                  pl.BlockSpec((1, 1, N_MOD * d), tok.mod_index(tm)),
                  pl.BlockSpec((1, d), lambda t: (0, 0))]
                 + [_resident(w.shape) for w in weights],
        out_specs=[pl.BlockSpec((tm, w.shape[1]), lambda t: (t, 0)) for w in weights],
        out_shape=[jax.ShapeDtypeStruct((rows, w.shape[1]), F32) for w in weights],
        compiler_params=_cparams(("parallel",)),
        name="mix_in",
    )(h, mods, g.reshape(1, d), *weights)


def _feat_kernel(z_ref, zp_ref, zn_ref, mu_ref, wup_ref, aup_ref, gup_ref, w0_ref, a0_ref, kk_ref, ka_ref, rk_ref,
                 ones_ref, r_out, v_out, kap_out, lw_out, kd_out, b_out, g_out, bonus_out, *, lat_tiles, d_rwkv):
    i = pl.program_id(1)
    tt = z_ref.shape[0]
    is_ctx = i == lat_tiles
    z = z_ref[...]
    first_of_seq = jnp.logical_or(i == 0, is_ctx)
    last_of_seq = jnp.logical_or(i == lat_tiles - 1, is_ctx)
    prev_row = jnp.where(first_of_seq, 0.0, zp_ref[7:8, :])
    next_row = jnp.where(last_of_seq, 0.0, zn_ref[0:1, :])
    row = lax.broadcasted_iota(jnp.int32, (tt, 1), 0)
    z_prev = jnp.where(row == 0, prev_row, pltpu.roll(z, 1, axis=0))
    z_next = jnp.where(row == tt - 1, next_row, pltpu.roll(z, tt - 1, axis=0))
    z = z + mu_ref[...] * (0.5 * (z_prev + z_next) - z)

    dr = d_rwkv
    r, k, v = z[:, :dr], z[:, dr:2 * dr], z[:, 2 * dr:3 * dr]
    low = z[:, 3 * dr:3 * dr + DECAY_RANK + ICLR_RANK]
    gd = z[:, 3 * dr + DECAY_RANK + ICLR_RANK:]
    ones = ones_ref[...]

    kap = k * kk_ref[...]
    kap = kap * lax.rsqrt(jnp.maximum(_lane_block_dot_x2(kap * kap, ones), KAP_NORM_EPS * KAP_NORM_EPS))
    tanh_low = jnp.tanh(low)
    low_bf16 = low.astype(BF16)
    kd_sum = jnp.zeros_like(k)
    for dd in range(2):
        lw_out[dd, 0] = -DECAY_SCALE * _sigmoid(w0_ref[dd] + _dot_x2(tanh_low, wup_ref[dd]))
        a = _sigmoid(a0_ref[dd] + _dot(low_bf16, aup_ref[dd]))
        kd = k * (1.0 + (a - 1.0) * ka_ref[...])
        kd_out[dd, 0] = kd.astype(kd_out.dtype)
        b_out[dd, 0] = (kap * a).astype(b_out.dtype)
        kd_sum = kd_sum + kd
    r_out[0] = r.astype(r_out.dtype)
    v_out[0] = v.astype(v_out.dtype)
    kap_out[0] = kap.astype(kap_out.dtype)
    g_out[0] = _dot(_sigmoid(gd).astype(BF16), gup_ref[...])
    bonus_out[0] = _lane_block_dot_x2(r * kd_sum * rk_ref[...], ones) * v


def _rwkv_features(tok, za, p):
    batch, n, nc = tok.batch, tok.n_lat, tok.n_ctx
    t = n + nc
    da = za.shape[1]
    dr = p["k_k"].shape[-1]
    tt = FEAT_TILE
    lat_tiles = n // tt
    ctx_block = tok.t_lat // tt
    hb = tt // 8
    n_hblk = za.shape[0] // 8
    row = lambda name: p[name].reshape(1, dr)
    tile = lambda b, i: jnp.where(i < lat_tiles, b * lat_tiles + i, ctx_block + b)
    seq_spec = pl.BlockSpec((1, tt, dr), lambda b, i: (b, i, 0))
    dir_spec = pl.BlockSpec((2, 1, tt, dr), lambda b, i: (0, b, i, 0))
    full = lambda a: pl.BlockSpec(a.shape, lambda b, i: (0,) * a.ndim)
    consts = [p["mu"].reshape(1, da), p["w_up_pad"], p["a_up_pad"], p["g_up"],
              p["w0"].reshape(2, 1, dr), p["a0"].reshape(2, 1, dr), row("k_k"), row("k_a"), row("r_k"),
              p["head_ones"]]
    seq = lambda dt: jax.ShapeDtypeStruct((batch, t, dr), dt)
    dirs = lambda dt: jax.ShapeDtypeStruct((2, batch, t, dr), dt)
    return pl.pallas_call(
        functools.partial(_feat_kernel, lat_tiles=lat_tiles, d_rwkv=dr),
        grid=(batch, t // tt),
        in_specs=[pl.BlockSpec((tt, da), lambda b, i: (tile(b, i), 0)),
                  pl.BlockSpec((8, da), lambda b, i: (jnp.maximum(tile(b, i) * hb - 1, 0), 0)),
                  pl.BlockSpec((8, da), lambda b, i: (jnp.minimum((tile(b, i) + 1) * hb, n_hblk - 1), 0))]
                 + [full(a) for a in consts],
        out_specs=[seq_spec, seq_spec, seq_spec, dir_spec, dir_spec, dir_spec, seq_spec, seq_spec],
        out_shape=[seq(BF16), seq(BF16), seq(BF16), dirs(F32), dirs(BF16), dirs(BF16), seq(F32), seq(F32)],
        compiler_params=_cparams(("parallel", "parallel")),
        name="rwkv_features",
    )(za, za, za, *consts)


def _chunk_masks(ch, rev):
    ri = lax.broadcasted_iota(jnp.int32, (PAIR, PAIR), 0)
    ci = lax.broadcasted_iota(jnp.int32, (PAIR, PAIR), 1)
    same_head = (ri // ch) == (ci // ch)
    rt, ct = ri % ch, ci % ch
    before, upto = (ct > rt, ct >= rt) if rev else (ct < rt, ct <= rt)
    return dict(same_head=same_head, diag=ri == ci,
                strict=jnp.logical_and(same_head, before), incl=jnp.logical_and(same_head, upto))


def _scan_kernel(rf_ref, vf_ref, kapf_ref, lwf_ref, kdf_ref, bf_ref, rb_ref, vb_ref, kapb_ref, lwb_ref, kdb_ref,
                 bb_ref, yf_ref, yb_ref, s_scr):
    c = pl.program_id(1)
    ch = CHUNK
    n_sub = rf_ref.shape[1] // ch
    n_pairs = rf_ref.shape[2] // PAIR

    @pl.when(c == 0)
    def _():
        s_scr[...] = jnp.zeros_like(s_scr)

    row_id = lax.broadcasted_iota(jnp.int32, (ch, 1), 0)
    lane_head = lax.broadcasted_iota(jnp.int32, (ch, PAIR), 1) // HEAD
    head0, head1 = lane_head == 0, lane_head == 1

    def stack_heads(x):
        return jnp.concatenate([jnp.where(head0, x, 0.0), jnp.where(head1, x, 0.0)], axis=0)

    def fold_heads(x):
        return x[:ch] + x[ch:]

    chains = []
    dirs = ((rf_ref, vf_ref, kapf_ref, lwf_ref, kdf_ref, bf_ref, yf_ref),
            (rb_ref, vb_ref, kapb_ref, lwb_ref, kdb_ref, bb_ref, yb_ref))
    for d, (r_ref, v_ref, kap_ref, lw_ref, kd_ref, b_ref, y_ref) in enumerate(dirs):
        mk = _chunk_masks(ch, rev=(d == 1))
        for sub in (range(n_sub) if d == 0 else reversed(range(n_sub))):
            rows = slice(sub * ch, (sub + 1) * ch)
            lw_all = lw_ref[0, 0, rows, :]
            cl_all = lw_all
            for step in (1 << k for k in range(int(np.log2(ch)))):
                if d == 0:
                    cl_all = cl_all + jnp.where(row_id >= step, pltpu.roll(cl_all, step, axis=0), 0.0)
                else:
                    cl_all = cl_all + jnp.where(row_id < ch - step, pltpu.roll(cl_all, ch - step, axis=0), 0.0)
            for p in range(n_pairs):
                sl = slice(p * PAIR, (p + 1) * PAIR)
                f32 = lambda ref, *lead: ref[(*lead, rows, sl)].astype(F32)
                chains.append(dict(d=d, p=p, sl=sl, rows=rows, mk=mk, y_ref=y_ref, lw=lw_all[:, sl], cl=cl_all[:, sl],
                                   r=f32(r_ref, 0), v=f32(v_ref, 0), kap=f32(kap_ref, 0),
                                   kd=f32(kd_ref, 0, 0), b=f32(b_ref, 0, 0)))

    for q in chains:
        lw, cl, mk = q["lw"], q["cl"], q["mk"]
        tot = jnp.sum(lw, axis=0, keepdims=True)
        g_inv, g_rem = jnp.exp(-cl), jnp.exp(tot - cl)
        kq = stack_heads(q["kap"] * jnp.exp(cl - lw))
        rq = stack_heads(q["r"] * jnp.exp(cl))
        kdd, bdd = q["kd"] * g_inv, q["b"] * g_inv
        gram = _dot_nt(jnp.concatenate([kq, rq], axis=0).astype(BF16),
                       jnp.concatenate([bdd, bdd, kdd, kdd], axis=0).astype(BF16))
        q.update(kq=kq, rq=rq, g_tot=jnp.exp(tot),
                 kb=jnp.concatenate([q["kd"] * g_rem, q["b"] * g_rem], axis=0).astype(BF16),
                 l_b=jnp.where(mk["strict"], gram[:PAIR, :PAIR], 0.0),
                 l_k=jnp.where(mk["strict"], gram[:PAIR, PAIR:], 0.0),
                 pkb=jnp.concatenate([jnp.where(mk["incl"], gram[PAIR:, PAIR:], 0.0),
                                      jnp.where(mk["incl"], -gram[PAIR:, :PAIR], 0.0)], axis=1).astype(BF16))
    for q in chains:
        q["v2"] = jnp.concatenate([q["v"], q["v"]], axis=0).astype(BF16)
        lkv = jnp.where(q["mk"]["same_head"], _dot(q["l_k"].astype(BF16), q["v2"]), 0.0)
        q["rhs"] = jnp.concatenate([q["kq"], lkv], axis=1).astype(BF16)
        q["lb"] = q["l_b"].astype(BF16)
    for q in chains:
        q["lp"] = _dot(q["lb"], q["lb"]).astype(BF16)
        q["t"] = jnp.where(q["mk"]["diag"], 1.0, 0.0) - q["l_b"]
    n_sq = int(np.log2(ch))
    for k in range(1, n_sq):
        for q in chains:
            w = jnp.concatenate([q["t"].astype(BF16), q["lp"]], axis=1) if k < n_sq - 1 else q["t"].astype(BF16)
            q["both"] = _dot(q["lp"], w)
        for q in chains:
            q["t"] = q["t"] + q["both"][:, :PAIR]
            if k < n_sq - 1:
                q["lp"] = q["both"][:, PAIR:].astype(BF16)
    for q in chains:
        q["x"] = _dot(q["t"].astype(BF16), q["rhs"])
    for q in chains:
        x = q["x"]
        mk = q["mk"]
        xk, w1 = x[:, :PAIR], x[:, PAIR:]
        rhs = jnp.concatenate([jnp.concatenate([q["v2"], jnp.zeros_like(q["v2"])], axis=1),
                               jnp.concatenate([w1, xk], axis=1).astype(BF16)], axis=0)
        big = _dot(q["pkb"], rhs)
        q["y0"] = fold_heads(jnp.where(mk["same_head"], big[:, :PAIR], 0.0))
        q["r_eff"] = (q["rq"] + big[:, PAIR:]).astype(BF16)
        lhs = jnp.concatenate([jnp.concatenate([q["v"], -fold_heads(w1)], axis=0),
                               jnp.concatenate([jnp.zeros_like(q["v"]), -fold_heads(xk)], axis=0)], axis=1)
        upd = _dot(lhs.T.astype(BF16), q["kb"])
        q["z0"] = jnp.where(mk["same_head"], upd[:PAIR], 0.0)
        q["a"] = jnp.where(mk["diag"], q["g_tot"], jnp.where(mk["same_head"], upd[PAIR:], 0.0)).astype(BF16)
    for q in chains:
        s0 = s_scr[q["d"], q["p"]].astype(BF16)
        q["y_ref"][0, q["rows"], q["sl"]] =q["y0"] + fold_heads(_dot_nt(q["r_eff"], s0))
        s_scr[q["d"], q["p"]] = _dot(s0, q["a"]) + q["z0"]


def _rwkv_scan(r, v, kap, lw, kd, b, n_ctx):
    batch, t, dr = r.shape
    rows = SCAN_CHUNKS * CHUNK
    assert t % rows == 0 and n_ctx % rows == 0
    n_blocks, ctx_blocks = t // rows, n_ctx // rows
    lat_blocks = n_blocks - ctx_blocks

    def fwd_block(c):
        return jnp.where(c < ctx_blocks, lat_blocks + c, c - ctx_blocks)

    def bwd_block(c):
        return n_blocks - 1 - c

    fwd_seq = pl.BlockSpec((1, rows, dr), lambda bb, c: (bb, fwd_block(c), 0))
    bwd_seq = pl.BlockSpec((1, rows, dr), lambda bb, c: (bb, bwd_block(c), 0))
    fwd_dir = pl.BlockSpec((1, 1, rows, dr), lambda bb, c: (0, bb, fwd_block(c), 0))
    bwd_dir = pl.BlockSpec((1, 1, rows, dr), lambda bb, c: (1, bb, bwd_block(c), 0))
    y_shape = jax.ShapeDtypeStruct((batch, t, dr), F32)
    return pl.pallas_call(
        _scan_kernel,
        grid=(batch, n_blocks),
        in_specs=[fwd_seq, fwd_seq, fwd_seq, fwd_dir, fwd_dir, fwd_dir,
                  bwd_seq, bwd_seq, bwd_seq, bwd_dir, bwd_dir, bwd_dir],
        out_specs=[fwd_seq, bwd_seq],
        out_shape=[y_shape, y_shape],
        scratch_shapes=[pltpu.VMEM((2, dr // PAIR, PAIR, PAIR), F32)],
        compiler_params=_cparams(("parallel", "arbitrary")),
        name="rwkv_scan",
    )(r, v, kap, lw, kd, b, r, v, kap, lw, kd, b)


def _even_out_kernel(h_ref, m_ref, yf_ref, yb_ref, g_ref, bonus_ref, fl_ref, fc_ref, gnw_ref, gnb_ref, mean_ref,
                     wo_ref, wf_ref, o_ref, *, lat_tiles):
    d = h_ref.shape[1]
    y = yf_ref[0] + yb_ref[0]
    mean_mat = mean_ref[...]
    dev = y - _lane_block_dot_x2(y, mean_mat)
    var = _lane_block_dot_x2(dev * dev, mean_mat)
    yn = dev * lax.rsqrt(var + GN_EPS) * gnw_ref[...] + gnb_ref[...]
    o = ((yn + bonus_ref[0]) * g_ref[0]).astype(BF16)
    f = jnp.where(pl.program_id(0) < lat_tiles, fl_ref[...], fc_ref[...]).astype(BF16)
    mixed = _dot(o, wo_ref[...]) + _dot(f, wf_ref[...])
    o_ref[...] = h_ref[...] + m_ref[0, :, 5 * d:6 * d] * mixed


def _even_out(tok, h, mods, seq_arrays, f_lat, f_ctx, p):
    d, batch, n = tok.d, tok.batch, tok.n_lat
    dr = p["gn_w"].shape[-1]
    tt = FEAT_TILE
    per_batch = n // tt
    lat_tiles = batch * per_batch
    seq_spec = pl.BlockSpec((1, tt, dr), lambda t: (jnp.where(t < lat_tiles, t // per_batch, t - lat_tiles),
                                                    jnp.where(t < lat_tiles, t % per_batch, per_batch), 0))
    df = f_lat.shape[-1]
    vec = lambda a: pl.BlockSpec((1, a.shape[-1]), lambda t: (0, 0))
    return pl.pallas_call(
        functools.partial(_even_out_kernel, lat_tiles=lat_tiles),
        grid=(tok.t_all // tt,),
        in_specs=[pl.BlockSpec((tt, d), lambda t: (t, 0)),
                  pl.BlockSpec((1, 1, N_MOD * d), tok.mod_index(tt))]
                 + [seq_spec] * len(seq_arrays)
                 + [pl.BlockSpec((tt, df), lambda t: (jnp.minimum(t, lat_tiles - 1), 0)),
                    pl.BlockSpec((tt, df), lambda t: (jnp.maximum(t - lat_tiles, 0), 0)),
                    vec(p["gn_w"]), vec(p["gn_b"]), _resident(p["head_mean"].shape),
                    _resident(p["w_out_o"].shape), _resident(p["w_out_f"].shape)],
        out_specs=pl.BlockSpec((tt, d), lambda t: (t, 0)),
        out_shape=jax.ShapeDtypeStruct((tok.t_all, d), F32),
        compiler_params=_cparams(("parallel",)),
        name="even_out",
    )(h, mods, *seq_arrays, f_lat.reshape(-1, df), f_ctx.reshape(-1, df), p["gn_w"].reshape(1, dr),
      p["gn_b"].reshape(1, dr), p["head_mean"], p["w_out_o"], p["w_out_f"])


def _fourier_kernel(cs_ref, u_ref, cg_ref, sg_ref, o_ref, stacked_scr):
    n = u_ref.shape[0]

    @pl.when(pl.program_id(1) == 0)
    def _():
        u_hi, u_lo = _split2(u_ref[...])
        stacked_scr[:n] = (_dot(u_hi, cg_ref[...]) + _dot(u_lo, cg_ref[...])).astype(BF16)
        stacked_scr[n:] = (_dot(u_hi, sg_ref[...]) + _dot(u_lo, sg_ref[...])).astype(BF16)

    o_ref[0] = _dot(cs_ref[...], stacked_scr[...])


def _dft_tables(n, groups, col_major):
    n_hi = n // GRID_W
    idx = jnp.arange(n, dtype=jnp.int32)
    pos = (idx % n_hi) * GRID_W + idx // n_hi if col_major else idx
    k = pos[:, None]
    ang_hi = ((k * (jnp.arange(n_hi, dtype=jnp.int32) * GRID_W)[None, :]) % n).astype(F32) * (2.0 * np.pi / n)
    ang_lo = ((k * jnp.arange(GRID_W, dtype=jnp.int32)[None, :]) % n).astype(F32) * (2.0 * np.pi / n)
    scale = 1.0 / np.sqrt(n * FOURIER_GROUP)
    c_hi, s_hi = jnp.cos(ang_hi) * scale, jnp.sin(ang_hi) * scale
    c_lo, s_lo = jnp.cos(ang_lo), jnp.sin(ang_lo)
    if col_major:
        hi, lo = (lambda a: a[:, None, :]), (lambda a: a[:, :, None])
    else:
        hi, lo = (lambda a: a[:, :, None]), (lambda a: a[:, None, :])
    cos = (hi(c_hi) * lo(c_lo) - hi(s_hi) * lo(s_lo)).reshape(n, n)
    sin = (hi(s_hi) * lo(c_lo) + hi(c_hi) * lo(s_lo)).reshape(n, n)
    cs = jnp.concatenate([cos, -sin], axis=1).astype(BF16)
    kg = np.arange(FOURIER_GROUP)
    ang_g = 2.0 * np.pi * ((kg[:, None] * kg[None, :]) % FOURIER_GROUP) / FOURIER_GROUP
    eye = np.eye(groups)
    cg = jnp.asarray(np.kron(eye, np.cos(ang_g)), F32).astype(BF16)
    sg = jnp.asarray(np.kron(eye, np.sin(ang_g)), F32).astype(BF16)
    return cs, cg, sg


def _fourier_mix(zf, batch, n, first_block, col_major=False):
    ch = zf.shape[1]
    cs, cg, sg = _dft_tables(n, ch // FOURIER_GROUP, col_major)
    tn = min(n, 512)
    return pl.pallas_call(
        _fourier_kernel,
        grid=(batch, n // tn),
        in_specs=[pl.BlockSpec((tn, 2 * n), lambda b, i: (i, 0)),
                  pl.BlockSpec((n, ch), lambda b, i: (first_block + b, 0)),
                  pl.BlockSpec((ch, ch), lambda b, i: (0, 0)),
                  pl.BlockSpec((ch, ch), lambda b, i: (0, 0))],
        out_specs=pl.BlockSpec((1, tn, ch), lambda b, i: (b, i, 0)),
        out_shape=jax.ShapeDtypeStruct((batch, n, ch), F32),
        scratch_shapes=[pltpu.VMEM((2 * n, ch), BF16)],
        compiler_params=_cparams(("parallel", "arbitrary")),
        name="fourier_mix",
    )(cs, zf, cg, sg)


def _odd_kernel(h_ref, m_ref, z_ref, zp_ref, zn_ref, cw_ref, cb_ref, cg_ref, pw_ref, ps_ref, wu_ref, wp_ref,
                o_ref, u_scr, *, lat_tiles, lat_seq_tiles, n_lat, n_ctx, d_conv):
    i = pl.program_id(0)
    tt, d = h_ref.shape
    is_lat = i < lat_tiles
    seq_tile = jnp.where(is_lat, i % lat_seq_tiles, 0)
    seq_tiles = jnp.where(is_lat, lat_seq_tiles, n_ctx // tt)
    seq_len = jnp.where(is_lat, n_lat, n_ctx)
    zp = jnp.where(seq_tile == 0, 0.0, zp_ref[...])
    zn = jnp.where(seq_tile == seq_tiles - 1, 0.0, zn_ref[...])
    ext = jnp.concatenate([zp, z_ref[...], zn], axis=0)

    dc = d_conv
    rows = tt + 2 * HALO
    glu = ext[:, :dc] * _sigmoid(ext[:, dc:2 * dc])
    u_scr[0] = glu
    for r in range(1, 8):
        u_scr[r] = pltpu.roll(glu, rows - r, axis=0)
    acc = jnp.zeros((tt, dc), F32) + cb_ref[...]
    for k in range(CONV_WIDTH):
        shift = HALO - CONV_WIDTH // 2 + k
        acc = acc + cw_ref[k:k + 1, :] * u_scr[shift % 8, pl.ds(shift - shift % 8, tt), :]
    u = acc * lax.rsqrt(jnp.mean(acc * acc, axis=-1, keepdims=True) + NORM_EPS) * cg_ref[...]
    u = _silu(u)

    q = ext[:, 2 * dc:]
    sums, win, step = [], q, 1
    for width in POOL_WIDTHS:
        if width == 2:
            win = q + pltpu.roll(q, 1, axis=0)
        else:
            win = pltpu.roll(win, step, axis=0) + pltpu.roll(win, rows - step, axis=0)
            step *= 2
        sums.append(win[HALO:HALO + tt])
    n_groups = len(POOL_WIDTHS)
    group = lax.broadcasted_iota(jnp.int32, (tt, q.shape[1]), 1) // (q.shape[1] // n_groups)
    pos = seq_tile * tt + lax.broadcasted_iota(jnp.int32, (tt, q.shape[1]), 0)
    total = sums[-1]
    width_of = jnp.full(group.shape, POOL_WIDTHS[-1], jnp.int32)
    for gi in range(n_groups - 2, -1, -1):
        total = jnp.where(group == gi, sums[gi], total)
        width_of = jnp.where(group == gi, POOL_WIDTHS[gi], width_of)
    lo = jnp.maximum(pos - width_of // 2, 0)
    hi = jnp.minimum(pos + (width_of - 1 - width_of // 2), seq_len - 1)
    pooled = total / (hi - lo + 1).astype(F32) - q[HALO:HALO + tt]
    pooled = _dot(pooled.astype(BF16), pw_ref[...]) * ps_ref[...]

    mixed = _dot(u.astype(BF16), wu_ref[...]) + _dot(pooled.astype(BF16), wp_ref[...])
    o_ref[...] = h_ref[...] + m_ref[0, :, 5 * d:6 * d] * mixed


def _odd_mixer(tok, h, mods, z, p, *, with_ctx):
    d, d_in = tok.d, z.shape[1]
    dc = p["conv_w"].shape[1]
    dp = d_in - 2 * dc
    tt = 256
    rows = tok.rows(with_ctx)
    hb, n_hblk = tt // HALO, z.shape[0] // HALO
    full = lambda a: pl.BlockSpec(a.shape, lambda i: (0,) * a.ndim)
    consts = [p["conv_w"], p["conv_b"].reshape(1, dc), p["cnorm_g"].reshape(1, dc), p["pool_w_bd"],
              p["pool_scale"].reshape(1, dp), p["w_out_u"], p["w_out_p"]]
    return pl.pallas_call(
        functools.partial(_odd_kernel, lat_tiles=tok.t_lat // tt, lat_seq_tiles=tok.n_lat // tt,
                          n_lat=tok.n_lat, n_ctx=tok.n_ctx, d_conv=dc),
        grid=(rows // tt,),
        in_specs=[pl.BlockSpec((tt, d), lambda i: (i, 0)),
                  pl.BlockSpec((1, 1, N_MOD * d), tok.mod_index(tt)),
                  pl.BlockSpec((tt, d_in), lambda i: (i, 0)),
                  pl.BlockSpec((HALO, d_in), lambda i: (jnp.maximum(i * hb - 1, 0), 0)),
                  pl.BlockSpec((HALO, d_in), lambda i: (jnp.minimum((i + 1) * hb, n_hblk - 1), 0))]
                 + [full(a) for a in consts],
        out_specs=pl.BlockSpec((tt, d), lambda i: (i, 0)),
        out_shape=jax.ShapeDtypeStruct((rows, d), F32),
        scratch_shapes=[pltpu.VMEM((8, tt + 2 * HALO, dc), F32)],
        compiler_params=_cparams(("parallel",)),
        name="odd_mixer",
    )(h, mods, z, z, z, *consts)


def _even_mixer(tok, h, mods, norm_g, p, col_major):
    za, zf = _mix_in(tok, h, mods, norm_g, [p["w_in_a"], p["w_in_f"]], with_ctx=True)
    f_lat = _fourier_mix(zf, tok.batch, tok.n_lat, 0, col_major)
    f_ctx = _fourier_mix(zf, tok.batch, tok.n_ctx, tok.t_lat // tok.n_ctx)
    r, v, kap, lw, kd, bb, g, bonus = _rwkv_features(tok, za, p)
    y_fwd, y_bwd = _rwkv_scan(r, v, kap, lw, kd, bb, tok.n_ctx)
    return _even_out(tok, h, mods, [y_fwd, y_bwd, g, bonus], f_lat, f_ctx, p)


def _pad_rows(w, before, total):
    return jnp.pad(w, ((0, 0), (before, total - before - w.shape[1]), (0, 0)))


def _block_diag(blocks):
    g, a, b = blocks.shape
    eye = jnp.eye(g, dtype=blocks.dtype)
    return (eye[:, None, :, None] * blocks[:, :, None, :]).reshape(g * a, g * b)


def kernel(x, c, ctx, c_ctx, ada_w, ada_b, norm_g, ffn1_w_gu, ffn1_w_down, ffn2_w_gu, ffn2_w_down,
           e_w_in, e_mu, e_w0, e_w_up, e_a0, e_a_up, e_g_up, e_k_k, e_k_a, e_r_k, e_gn_w, e_gn_b, e_w_out,
           o_w_in, o_conv_w, o_conv_b, o_cnorm_g, o_pool_w, o_pool_scale, o_w_out, final_g):
    batch, n_lat, d = x.shape
    n_ctx = ctx.shape[1]
    depth = ada_w.shape[0]
    tok = _Tokens(batch, n_lat, n_ctx, d)
    assert batch < MOD_ROWS and n_lat % 512 == 0 and (batch * n_ctx) % 512 == 0 and n_ctx == FEAT_TILE
    assert n_lat % (GRID_W * 8) == 0 and FEAT_TILE % (n_lat // GRID_W) == 0 and FEAT_TILE % GRID_W == 0

    cc = jnp.zeros((MOD_ROWS, d), F32).at[:batch].set(c).at[batch].set(c_ctx)
    mods_all = _ada_table(cc, ada_w, ada_b).reshape(depth, MOD_ROWS, 1, N_MOD * d)
    h = jnp.concatenate([x.reshape(batch * n_lat, d), ctx.reshape(batch * n_ctx, d)], axis=0)

    dr = e_k_k.shape[-1]
    low_rank = DECAY_RANK + ICLR_RANK
    head_ones = _block_diag(jnp.ones((HEAD_SUM_LANES // HEAD, HEAD, HEAD), F32))
    n_rows = n_lat // GRID_W
    for i in range(depth):
        j = i // 2
        with_ctx = not (i == depth - 1 and i % 2 == 1)
        col_major = i % 2 == 0 and j % 2 == 1
        mods = mods_all[i]
        h = _half_ffn(tok, h, mods, norm_g[i, 0], ffn1_w_gu[i].astype(BF16), ffn1_w_down[i].astype(BF16), final_g,
                      off=0, with_ctx=with_ctx, regroup=GRID_W if col_major else None)
        if i % 2 == 0:
            da = e_mu.shape[-1]
            p = dict(w_in_a=e_w_in[j, :, :da].astype(BF16), w_in_f=e_w_in[j, :, da:].astype(BF16),
                     mu=e_mu[j], w0=e_w0[j], a0=e_a0[j], g_up=e_g_up[j].astype(BF16),
                     w_up_pad=_pad_rows(e_w_up[j], 0, low_rank).astype(BF16),
                     a_up_pad=_pad_rows(e_a_up[j], DECAY_RANK, low_rank).astype(BF16),
                     k_k=e_k_k[j], k_a=e_k_a[j], r_k=e_r_k[j], gn_w=e_gn_w[j], gn_b=e_gn_b[j],
                     w_out_o=e_w_out[j, :dr].astype(BF16), w_out_f=e_w_out[j, dr:].astype(BF16),
                     head_ones=head_ones.astype(BF16),
                     head_mean=(head_ones / HEAD).astype(BF16))
            h = _even_mixer(tok, h, mods, norm_g[i, 1], p, col_major)
        else:
            dc = o_conv_w.shape[-1]
            p = dict(conv_w=o_conv_w[j], conv_b=o_conv_b[j], cnorm_g=o_cnorm_g[j],
                     pool_w_bd=_block_diag(o_pool_w[j]).astype(BF16), pool_scale=o_pool_scale[j],
                     w_out_u=o_w_out[j, :dc].astype(BF16), w_out_p=o_w_out[j, dc:].astype(BF16))
            (z,) = _mix_in(tok, h, mods, norm_g[i, 1], [o_w_in[j].astype(BF16)], with_ctx=with_ctx)
            h = _odd_mixer(tok, h, mods, z, p, with_ctx=with_ctx)
        h = _half_ffn(tok, h, mods, norm_g[i, 2], ffn2_w_gu[i].astype(BF16), ffn2_w_down[i].astype(BF16), final_g,
                      off=6, with_ctx=with_ctx, final=(i == depth - 1), regroup=n_rows if col_major else None)
    return h[:tok.t_lat].reshape(batch, n_lat, d)
```

```python
import functools

import jax
import jax.numpy as jnp
import numpy as np
from jax import lax
from jax.experimental import pallas as pl
from jax.experimental.pallas import tpu as pltpu

F32 = jnp.float32
BF16 = jnp.bfloat16

N_MOD = 9
NORM_EPS = 1e-6
GRID_W = 64
HEAD = 64
PAIR = 2 * HEAD
DECAY_RANK = 64
ICLR_RANK = 64
GATE_RANK = 128
DECAY_SCALE = 0.606531
GN_EPS = 64e-5
KAP_NORM_EPS = 1e-12
HEAD_SUM_LANES = 256
FOURIER_GROUP = 64
FOURIER_BATCH = 2
CONV_WIDTH = 31
POOL_WIDTHS = (2, 4, 8, 16)
CHUNK = 64
SCAN_CHUNKS = 4
HALO = 16
FEAT_TILE = 256
MOD_ROWS = 16
VMEM_LIMIT = 56 * 1024 * 1024


def _cparams(sem):
    return pltpu.CompilerParams(dimension_semantics=sem, vmem_limit_bytes=VMEM_LIMIT)


def _dot(a, b):
    return jnp.dot(a, b, preferred_element_type=F32)


def _dot_nt(a, b):
    return lax.dot_general(a, b, (((1,), (1,)), ((), ())), preferred_element_type=F32)


def _split2(x):
    hi = x.astype(BF16)
    lo = (x - hi.astype(F32)).astype(BF16)
    return hi, lo


def _dot_x2(x, w_bf16):
    hi, lo = _split2(x)
    return _dot(hi, w_bf16) + _dot(lo, w_bf16)


def _lane_block_dot_x2(x, w_bf16):
    hi, lo = _split2(x)
    n = w_bf16.shape[0]
    return jnp.concatenate([_dot(hi[:, j:j + n], w_bf16) + _dot(lo[:, j:j + n], w_bf16)
                            for j in range(0, x.shape[1], n)], axis=1)


def _dot_x3(x, w):
    xh, xl = _split2(x)
    wh, wl = _split2(w)
    return _dot(xh, wh) + (_dot(xh, wl) + _dot(xl, wh))


def _sigmoid(x):
    return 0.5 * jnp.tanh(0.5 * x) + 0.5


def _silu(x):
    return x * _sigmoid(x)


def _modulate(h, g, shift, scale):
    y = h * lax.rsqrt(jnp.mean(h * h, axis=-1, keepdims=True) + NORM_EPS)
    return y * g * (1.0 + scale) + shift


def _ada_kernel(c_ref, w_ref, b_ref, o_ref):
    o_ref[0] = _dot_x3(_silu(c_ref[...]), w_ref[0]) + b_ref[0]


def _ada_table(cc, ada_w, ada_b):
    depth, d, nd = ada_w.shape
    tn = 1024
    return pl.pallas_call(
        _ada_kernel,
        grid=(depth, nd // tn),
        in_specs=[pl.BlockSpec((MOD_ROWS, d), lambda i, j: (0, 0)),
                  pl.BlockSpec((1, d, tn), lambda i, j: (i, 0, j)),
                  pl.BlockSpec((1, 1, tn), lambda i, j: (i, 0, j))],
        out_specs=pl.BlockSpec((1, MOD_ROWS, tn), lambda i, j: (i, 0, j)),
        out_shape=jax.ShapeDtypeStruct((depth, MOD_ROWS, nd), F32),
        compiler_params=_cparams(("parallel", "parallel")),
        name="ada_table",
    )(cc, ada_w, ada_b.reshape(depth, 1, nd))


class _Tokens:
    def __init__(self, batch, n_lat, n_ctx, d):
        self.batch, self.n_lat, self.n_ctx, self.d = batch, n_lat, n_ctx, d
        self.t_lat = batch * n_lat
        self.t_all = batch * (n_lat + n_ctx)

    def rows(self, with_ctx):
        return self.t_all if with_ctx else self.t_lat

    def mod_index(self, tm):
        lat_tiles, per_batch, ctx_row = self.t_lat // tm, self.n_lat // tm, self.batch
        return lambda t, *_: (jnp.where(t < lat_tiles, t // per_batch, ctx_row), 0, 0)


def _mod_slices(m_ref, off, d):
    return [m_ref[0, :, (off + k) * d:(off + k + 1) * d] for k in range(3)]


def _resident(shape):
    return pl.BlockSpec(shape, lambda *_: (0,) * len(shape), pipeline_mode=pl.Buffered(1))


def _ffn_kernel(*refs, off, final, lat_tiles, alt):
    alt_ref = refs[0] if alt else None
    h_ref, m_ref, g_ref, wgu_ref, wd_ref, fg_ref, o_ref = refs[-7:]
    d, ff = h_ref.shape[1], wd_ref.shape[0]
    shift, scale, gate = _mod_slices(m_ref, off, d)
    h = h_ref[...]
    is_lat = pl.program_id(0) < lat_tiles
    if alt == "regroup":
        h = jnp.where(is_lat, jnp.concatenate([alt_ref[:, c, :] for c in range(alt_ref.shape[1])], axis=0), h)
    elif alt == "ctx":
        h = jnp.where(is_lat, h, alt_ref[...])
    y = _modulate(h, g_ref[...], shift, scale).astype(BF16)
    gu = _dot(y, wgu_ref[...])
    act = (_silu(gu[:, :ff]) * gu[:, ff:]).astype(BF16)
    out = h + (0.5 * gate) * _dot(act, wd_ref[...])
    if final:
        out = out * lax.rsqrt(jnp.mean(out * out, axis=-1, keepdims=True) + NORM_EPS) * fg_ref[...]
    o_ref[...] = out


def _half_ffn(tok, h, mods, g, w_gu, w_down, final_g, *, off, with_ctx, final=False, regroup=None, ctx_rows=None):
    d = tok.d
    tm = 512
    rows = tok.rows(with_ctx)
    lat_tiles = tok.t_lat // tm
    lat_only = lambda t: (jnp.minimum(t, lat_tiles - 1), 0)
    ctx_only = lambda t: (jnp.maximum(t - lat_tiles, 0), 0)
    ins, specs, alt = [], [], None
    h_spec = pl.BlockSpec((tm, d), lambda t: (t, 0))
    if regroup is not None:
        alt = "regroup"
        groups = tm // (tok.n_lat // regroup)
        per_batch = regroup // groups
        ins.append(h.reshape(h.shape[0] // regroup, regroup, d))
        specs.append(pl.BlockSpec((tm // groups, groups, d),
                                  lambda t: (lat_only(t)[0] // per_batch, lat_only(t)[0] % per_batch, 0)))
        h_spec = pl.BlockSpec((tm, d), lambda t: (jnp.maximum(t, lat_tiles), 0))
    elif ctx_rows is not None:
        alt = "ctx"
        ins.append(ctx_rows)
        specs.append(pl.BlockSpec((tm, d), ctx_only))
        h_spec = pl.BlockSpec((tm, d), lat_only)
    return pl.pallas_call(
        functools.partial(_ffn_kernel, off=off, final=final, lat_tiles=lat_tiles, alt=alt),
        grid=(rows // tm,),
        in_specs=specs + [h_spec,
                          pl.BlockSpec((1, 1, N_MOD * d), tok.mod_index(tm)),
                          pl.BlockSpec((1, d), lambda t: (0, 0)),
                          _resident(w_gu.shape),
                          _resident(w_down.shape),
                          pl.BlockSpec((1, d), lambda t: (0, 0))],
        out_specs=pl.BlockSpec((tm, d), lambda t: (t, 0)),
        out_shape=jax.ShapeDtypeStruct((rows, d), F32),
        compiler_params=_cparams(("parallel",)),
        name="half_ffn",
    )(*ins, h, mods, g.reshape(1, d), w_gu, w_down, final_g.reshape(1, d))


def _mix_in_kernel(h_ref, m_ref, g_ref, *refs):
    d = h_ref.shape[1]
    w_refs, z_refs = refs[:len(refs) // 2], refs[len(refs) // 2:]
    shift, scale, _ = _mod_slices(m_ref, 3, d)
    y = _modulate(h_ref[...], g_ref[...], shift, scale).astype(BF16)
    for w_ref, z_ref in zip(w_refs, z_refs):
        z_ref[...] = _dot(y, w_ref[...])


def _mix_in(tok, h, mods, g, weights, *, with_ctx):
    d = tok.d
    tm = 512
    rows = tok.rows(with_ctx)
    return pl.pallas_call(
        _mix_in_kernel,
        grid=(rows // tm,),
        in_specs=[pl.BlockSpec((tm, d), lambda t: (t, 0)),
                  pl.BlockSpec((1, 1, N_MOD * d), tok.mod_index(tm)),
                  pl.BlockSpec((1, d), lambda t: (0, 0))]
                 + [_resident(w.shape) for w in weights],
        out_specs=[pl.BlockSpec((tm, w.shape[1]), lambda t: (t, 0)) for w in weights],
        out_shape=[jax.ShapeDtypeStruct((rows, w.shape[1]), F32) for w in weights],
        compiler_params=_cparams(("parallel",)),
        name="mix_in",
    )(h, mods, g.reshape(1, d), *weights)


def _feat_kernel(z_ref, zp_ref, zn_ref, mu_ref, wup_ref, aup_ref, gup_ref, w0_ref, a0_ref, kk_ref, ka_ref, rk_ref,
                 ones_ref, r_out, v_out, kap_out, lw_out, kd_out, b_out, g_out, bonus_out, *, lat_tiles, d_rwkv):
    i = pl.program_id(1)
    tt = z_ref.shape[0]
    is_ctx = i == lat_tiles
    z = z_ref[...]
    first_of_seq = jnp.logical_or(i == 0, is_ctx)
    last_of_seq = jnp.logical_or(i == lat_tiles - 1, is_ctx)
    prev_row = jnp.where(first_of_seq, 0.0, zp_ref[7:8, :])
    next_row = jnp.where(last_of_seq, 0.0, zn_ref[0:1, :])
    row = lax.broadcasted_iota(jnp.int32, (tt, 1), 0)
    z_prev = jnp.where(row == 0, prev_row, pltpu.roll(z, 1, axis=0))
    z_next = jnp.where(row == tt - 1, next_row, pltpu.roll(z, tt - 1, axis=0))
    z = z + mu_ref[...] * (0.5 * (z_prev + z_next) - z)

    dr = d_rwkv
    r, k, v = z[:, :dr], z[:, dr:2 * dr], z[:, 2 * dr:3 * dr]
    low = z[:, 3 * dr:3 * dr + DECAY_RANK + ICLR_RANK]
    gd = z[:, 3 * dr + DECAY_RANK + ICLR_RANK:]
    ones = ones_ref[...]

    kap = k * kk_ref[...]
    kap = kap * lax.rsqrt(jnp.maximum(_lane_block_dot_x2(kap * kap, ones), KAP_NORM_EPS * KAP_NORM_EPS))
    tanh_low = jnp.tanh(low)
    low_bf16 = low.astype(BF16)
    kd_sum = jnp.zeros_like(k)
    for dd in range(2):
        lw_out[dd, 0] = -DECAY_SCALE * _sigmoid(w0_ref[dd] + _dot_x2(tanh_low, wup_ref[dd]))
        a = _sigmoid(a0_ref[dd] + _dot(low_bf16, aup_ref[dd]))
        kd = k * (1.0 + (a - 1.0) * ka_ref[...])
        kd_out[dd, 0] = kd.astype(kd_out.dtype)
        b_out[dd, 0] = (kap * a).astype(b_out.dtype)
        kd_sum = kd_sum + kd
    r_out[0] = r.astype(r_out.dtype)
    v_out[0] = v.astype(v_out.dtype)
    kap_out[0] = kap.astype(kap_out.dtype)
    g_out[0] = _dot(_sigmoid(gd).astype(BF16), gup_ref[...])
    bonus_out[0] = _lane_block_dot_x2(r * kd_sum * rk_ref[...], ones) * v


def _rwkv_features(tok, za, p):
    batch, n, nc = tok.batch, tok.n_lat, tok.n_ctx
    t = n + nc
    da = za.shape[1]
    dr = p["k_k"].shape[-1]
    tt = FEAT_TILE
    lat_tiles = n // tt
    ctx_block = tok.t_lat // tt
    hb = tt // 8
    n_hblk = za.shape[0] // 8
    row = lambda name: p[name].reshape(1, dr)
    tile = lambda b, i: jnp.where(i < lat_tiles, b * lat_tiles + i, ctx_block + b)
    seq_spec = pl.BlockSpec((1, tt, dr), lambda b, i: (b, i, 0))
    dir_spec = pl.BlockSpec((2, 1, tt, dr), lambda b, i: (0, b, i, 0))
    full = lambda a: pl.BlockSpec(a.shape, lambda b, i: (0,) * a.ndim)
    consts = [p["mu"].reshape(1, da), p["w_up_pad"], p["a_up_pad"], p["g_up"],
              p["w0"].reshape(2, 1, dr), p["a0"].reshape(2, 1, dr), row("k_k"), row("k_a"), row("r_k"),
              p["head_ones"]]
    seq = lambda dt: jax.ShapeDtypeStruct((batch, t, dr), dt)
    dirs = lambda dt: jax.ShapeDtypeStruct((2, batch, t, dr), dt)
    return pl.pallas_call(
        functools.partial(_feat_kernel, lat_tiles=lat_tiles, d_rwkv=dr),
        grid=(batch, t // tt),
        in_specs=[pl.BlockSpec((tt, da), lambda b, i: (tile(b, i), 0)),
                  pl.BlockSpec((8, da), lambda b, i: (jnp.maximum(tile(b, i) * hb - 1, 0), 0)),
                  pl.BlockSpec((8, da), lambda b, i: (jnp.minimum((tile(b, i) + 1) * hb, n_hblk - 1), 0))]
                 + [full(a) for a in consts],
        out_specs=[seq_spec, seq_spec, seq_spec, dir_spec, dir_spec, dir_spec, seq_spec, seq_spec],
        out_shape=[seq(BF16), seq(BF16), seq(BF16), dirs(F32), dirs(BF16), dirs(BF16), seq(F32), seq(F32)],
        compiler_params=_cparams(("parallel", "parallel")),
        name="rwkv_features",
    )(za, za, za, *consts)


def _chunk_masks(ch, rev):
    ri = lax.broadcasted_iota(jnp.int32, (PAIR, PAIR), 0)
    ci = lax.broadcasted_iota(jnp.int32, (PAIR, PAIR), 1)
    same_head = (ri // ch) == (ci // ch)
    rt, ct = ri % ch, ci % ch
    before, upto = (ct > rt, ct >= rt) if rev else (ct < rt, ct <= rt)
    return dict(same_head=same_head, diag=ri == ci,
                strict=jnp.logical_and(same_head, before), incl=jnp.logical_and(same_head, upto))


def _scan_kernel(rf_ref, vf_ref, kapf_ref, lwf_ref, kdf_ref, bf_ref, rb_ref, vb_ref, kapb_ref, lwb_ref, kdb_ref,
                 bb_ref, yf_ref, yb_ref, s_scr):
    c = pl.program_id(1)
    ch = CHUNK
    n_sub = rf_ref.shape[1] // ch
    n_pairs = rf_ref.shape[2] // PAIR

    @pl.when(c == 0)
    def _():
        s_scr[...] = jnp.zeros_like(s_scr)

    row_id = lax.broadcasted_iota(jnp.int32, (ch, 1), 0)
    lane_head = lax.broadcasted_iota(jnp.int32, (ch, PAIR), 1) // HEAD
    head0, head1 = lane_head == 0, lane_head == 1

    def stack_heads(x):
        return jnp.concatenate([jnp.where(head0, x, 0.0), jnp.where(head1, x, 0.0)], axis=0)

    def fold_heads(x):
        return x[:ch] + x[ch:]

    chains = []
    dirs = ((rf_ref, vf_ref, kapf_ref, lwf_ref, kdf_ref, bf_ref, yf_ref),
            (rb_ref, vb_ref, kapb_ref, lwb_ref, kdb_ref, bb_ref, yb_ref))
    for d, (r_ref, v_ref, kap_ref, lw_ref, kd_ref, b_ref, y_ref) in enumerate(dirs):
        mk = _chunk_masks(ch, rev=(d == 1))
        for sub in (range(n_sub) if d == 0 else reversed(range(n_sub))):
            rows = slice(sub * ch, (sub + 1) * ch)
            lw_all = lw_ref[0, 0, rows, :]
            cl_all = lw_all
            for step in (1 << k for k in range(int(np.log2(ch)))):
                if d == 0:
                    cl_all = cl_all + jnp.where(row_id >= step, pltpu.roll(cl_all, step, axis=0), 0.0)
                else:
                    cl_all = cl_all + jnp.where(row_id < ch - step, pltpu.roll(cl_all, ch - step, axis=0), 0.0)
            for p in range(n_pairs):
                sl = slice(p * PAIR, (p + 1) * PAIR)
                f32 = lambda ref, *lead: ref[(*lead, rows, sl)].astype(F32)
                chains.append(dict(d=d, p=p, sl=sl, rows=rows, mk=mk, y_ref=y_ref, lw=lw_all[:, sl], cl=cl_all[:, sl],
                                   r=f32(r_ref, 0), v=f32(v_ref, 0), kap=f32(kap_ref, 0),
                                   kd=f32(kd_ref, 0, 0), b=f32(b_ref, 0, 0)))

    for q in chains:
        lw, cl, mk = q["lw"], q["cl"], q["mk"]
        tot = jnp.sum(lw, axis=0, keepdims=True)
        g_inv, g_rem = jnp.exp(-cl), jnp.exp(tot - cl)
        kq = stack_heads(q["kap"] * jnp.exp(cl - lw))
        rq = stack_heads(q["r"] * jnp.exp(cl))
        kdd, bdd = q["kd"] * g_inv, q["b"] * g_inv
        gram = _dot_nt(jnp.concatenate([kq, rq], axis=0).astype(BF16),
                       jnp.concatenate([bdd, bdd, kdd, kdd], axis=0).astype(BF16))
        q.update(kq=kq, rq=rq, g_tot=jnp.exp(tot),
                 kb=jnp.concatenate([q["kd"] * g_rem, q["b"] * g_rem], axis=0).astype(BF16),
                 l_b=jnp.where(mk["strict"], gram[:PAIR, :PAIR], 0.0),
                 l_k=jnp.where(mk["strict"], gram[:PAIR, PAIR:], 0.0),
                 pkb=jnp.concatenate([jnp.where(mk["incl"], gram[PAIR:, PAIR:], 0.0),
                                      jnp.where(mk["incl"], -gram[PAIR:, :PAIR], 0.0)], axis=1).astype(BF16))
    for q in chains:
        q["v2"] = jnp.concatenate([q["v"], q["v"]], axis=0).astype(BF16)
        lkv = jnp.where(q["mk"]["same_head"], _dot(q["l_k"].astype(BF16), q["v2"]), 0.0)
        q["rhs"] = jnp.concatenate([q["kq"], lkv], axis=1).astype(BF16)
        q["lb"] = q["l_b"].astype(BF16)
    for q in chains:
        q["lp"] = _dot(q["lb"], q["lb"]).astype(BF16)
        q["t"] = jnp.where(q["mk"]["diag"], 1.0, 0.0) - q["l_b"]
    n_sq = int(np.log2(ch))
    for k in range(1, n_sq):
        for q in chains:
            w = jnp.concatenate([q["t"].astype(BF16), q["lp"]], axis=1) if k < n_sq - 1 else q["t"].astype(BF16)
            q["both"] = _dot(q["lp"], w)
        for q in chains:
            q["t"] = q["t"] + q["both"][:, :PAIR]
            if k < n_sq - 1:
                q["lp"] = q["both"][:, PAIR:].astype(BF16)
    for q in chains:
        q["x"] = _dot(q["t"].astype(BF16), q["rhs"])
    for q in chains:
        x = q["x"]
        mk = q["mk"]
        xk, w1 = x[:, :PAIR], x[:, PAIR:]
        rhs = jnp.concatenate([jnp.concatenate([q["v2"], jnp.zeros_like(q["v2"])], axis=1),
                               jnp.concatenate([w1, xk], axis=1).astype(BF16)], axis=0)
        big = _dot(q["pkb"], rhs)
        q["y0"] = fold_heads(jnp.where(mk["same_head"], big[:, :PAIR], 0.0))
        q["r_eff"] = (q["rq"] + big[:, PAIR:]).astype(BF16)
        lhs = jnp.concatenate([jnp.concatenate([q["v"], -fold_heads(w1)], axis=0),
                               jnp.concatenate([jnp.zeros_like(q["v"]), -fold_heads(xk)], axis=0)], axis=1)
        upd = _dot(lhs.T.astype(BF16), q["kb"])
        q["z0"] = jnp.where(mk["same_head"], upd[:PAIR], 0.0)
        q["a"] = jnp.where(mk["diag"], q["g_tot"], jnp.where(mk["same_head"], upd[PAIR:], 0.0)).astype(BF16)
    for q in chains:
        s0 = s_scr[q["d"], q["p"]].astype(BF16)
        q["y_ref"][0, q["rows"], q["sl"]] =q["y0"] + fold_heads(_dot_nt(q["r_eff"], s0))
        s_scr[q["d"], q["p"]] = _dot(s0, q["a"]) + q["z0"]


def _rwkv_scan(r, v, kap, lw, kd, b, n_ctx):
    batch, t, dr = r.shape
    rows = SCAN_CHUNKS * CHUNK
    assert t % rows == 0 and n_ctx % rows == 0
    n_blocks, ctx_blocks = t // rows, n_ctx // rows
    lat_blocks = n_blocks - ctx_blocks

    def fwd_block(c):
        return jnp.where(c < ctx_blocks, lat_blocks + c, c - ctx_blocks)

    def bwd_block(c):
        return n_blocks - 1 - c

    fwd_seq = pl.BlockSpec((1, rows, dr), lambda bb, c: (bb, fwd_block(c), 0))
    bwd_seq = pl.BlockSpec((1, rows, dr), lambda bb, c: (bb, bwd_block(c), 0))
    fwd_dir = pl.BlockSpec((1, 1, rows, dr), lambda bb, c: (0, bb, fwd_block(c), 0))
    bwd_dir = pl.BlockSpec((1, 1, rows, dr), lambda bb, c: (1, bb, bwd_block(c), 0))
    y_shape = jax.ShapeDtypeStruct((batch, t, dr), F32)
    return pl.pallas_call(
        _scan_kernel,
        grid=(batch, n_blocks),
        in_specs=[fwd_seq, fwd_seq, fwd_seq, fwd_dir, fwd_dir, fwd_dir,
                  bwd_seq, bwd_seq, bwd_seq, bwd_dir, bwd_dir, bwd_dir],
        out_specs=[fwd_seq, bwd_seq],
        out_shape=[y_shape, y_shape],
        scratch_shapes=[pltpu.VMEM((2, dr // PAIR, PAIR, PAIR), F32)],
        compiler_params=_cparams(("parallel", "arbitrary")),
        name="rwkv_scan",
    )(r, v, kap, lw, kd, b, r, v, kap, lw, kd, b)


def _even_out_kernel(h_ref, m_ref, yf_ref, yb_ref, g_ref, bonus_ref, fl_ref, fc_ref, gnw_ref, gnb_ref, mean_ref,
                     wo_ref, wf_ref, o_ref, *, lat_tiles):
    d = h_ref.shape[1]
    y = yf_ref[0] + yb_ref[0]
    mean_mat = mean_ref[...]
    dev = y - _lane_block_dot_x2(y, mean_mat)
    var = _lane_block_dot_x2(dev * dev, mean_mat)
    yn = dev * lax.rsqrt(var + GN_EPS) * gnw_ref[...] + gnb_ref[...]
    o = ((yn + bonus_ref[0]) * g_ref[0]).astype(BF16)
    f = jnp.where(pl.program_id(0) < lat_tiles, fl_ref[...], fc_ref[...]).astype(BF16)
    mixed = _dot(o, wo_ref[...]) + _dot(f, wf_ref[...])
    o_ref[...] = h_ref[...] + m_ref[0, :, 5 * d:6 * d] * mixed


def _even_out(tok, h, mods, seq_arrays, f_lat, f_ctx, p):
    d, batch, n = tok.d, tok.batch, tok.n_lat
    dr = p["gn_w"].shape[-1]
    tt = FEAT_TILE
    per_batch = n // tt
    lat_tiles = batch * per_batch
    seq_spec = pl.BlockSpec((1, tt, dr), lambda t: (jnp.where(t < lat_tiles, t // per_batch, t - lat_tiles),
                                                    jnp.where(t < lat_tiles, t % per_batch, per_batch), 0))
    df = f_lat.shape[-1]
    vec = lambda a: pl.BlockSpec((1, a.shape[-1]), lambda t: (0, 0))
    return pl.pallas_call(
        functools.partial(_even_out_kernel, lat_tiles=lat_tiles),
        grid=(tok.t_all // tt,),
        in_specs=[pl.BlockSpec((tt, d), lambda t: (t, 0)),
                  pl.BlockSpec((1, 1, N_MOD * d), tok.mod_index(tt))]
                 + [seq_spec] * len(seq_arrays)
                 + [pl.BlockSpec((tt, df), lambda t: (jnp.minimum(t, lat_tiles - 1), 0)),
                    pl.BlockSpec((tt, df), lambda t: (jnp.maximum(t - lat_tiles, 0), 0)),
                    vec(p["gn_w"]), vec(p["gn_b"]), _resident(p["head_mean"].shape),
                    _resident(p["w_out_o"].shape), _resident(p["w_out_f"].shape)],
        out_specs=pl.BlockSpec((tt, d), lambda t: (t, 0)),
        out_shape=jax.ShapeDtypeStruct((tok.t_all, d), F32),
        compiler_params=_cparams(("parallel",)),
        name="even_out",
    )(h, mods, *seq_arrays, f_lat.reshape(-1, df), f_ctx.reshape(-1, df), p["gn_w"].reshape(1, dr),
      p["gn_b"].reshape(1, dr), p["head_mean"], p["w_out_o"], p["w_out_f"])


def _fourier_kernel(cs_ref, u_ref, cg_ref, sg_ref, o_ref, stacked_scr):
    n, ch = u_ref.shape[0] // FOURIER_BATCH, u_ref.shape[1]

    @pl.when(pl.program_id(1) == 0)
    def _():
        for k in range(FOURIER_BATCH):
            u_hi, u_lo = _split2(u_ref[k * n:(k + 1) * n, :])
            cols = slice(k * ch, (k + 1) * ch)
            stacked_scr[:n, cols] = (_dot(u_hi, cg_ref[...]) + _dot(u_lo, cg_ref[...])).astype(BF16)
            stacked_scr[n:, cols] = (_dot(u_hi, sg_ref[...]) + _dot(u_lo, sg_ref[...])).astype(BF16)

    out = _dot(cs_ref[...], stacked_scr[...])
    for k in range(FOURIER_BATCH):
        o_ref[k] = out[:, k * ch:(k + 1) * ch]


def _dft_tables(n, groups, col_major):
    n_hi = n // GRID_W
    idx = jnp.arange(n, dtype=jnp.int32)
    pos = (idx % n_hi) * GRID_W + idx // n_hi if col_major else idx
    k = pos[:, None]
    ang_hi = ((k * (jnp.arange(n_hi, dtype=jnp.int32) * GRID_W)[None, :]) % n).astype(F32) * (2.0 * np.pi / n)
    ang_lo = ((k * jnp.arange(GRID_W, dtype=jnp.int32)[None, :]) % n).astype(F32) * (2.0 * np.pi / n)
    scale = 1.0 / np.sqrt(n * FOURIER_GROUP)
    c_hi, s_hi = jnp.cos(ang_hi) * scale, jnp.sin(ang_hi) * scale
    c_lo, s_lo = jnp.cos(ang_lo), jnp.sin(ang_lo)
    if col_major:
        hi, lo = (lambda a: a[:, None, :]), (lambda a: a[:, :, None])
    else:
        hi, lo = (lambda a: a[:, :, None]), (lambda a: a[:, None, :])
    cos = (hi(c_hi) * lo(c_lo) - hi(s_hi) * lo(s_lo)).reshape(n, n)
    sin = (hi(s_hi) * lo(c_lo) + hi(c_hi) * lo(s_lo)).reshape(n, n)
    cs = jnp.concatenate([cos, -sin], axis=1).astype(BF16)
    kg = np.arange(FOURIER_GROUP)
    ang_g = 2.0 * np.pi * ((kg[:, None] * kg[None, :]) % FOURIER_GROUP) / FOURIER_GROUP
    eye = np.eye(groups)
    cg = jnp.asarray(np.kron(eye, np.cos(ang_g)), F32).astype(BF16)
    sg = jnp.asarray(np.kron(eye, np.sin(ang_g)), F32).astype(BF16)
    return cs, cg, sg


def _fourier_mix(zf, batch, n, first_block, col_major=False):
    ch = zf.shape[1]
    cs, cg, sg = _dft_tables(n, ch // FOURIER_GROUP, col_major)
    tn = min(n, 512)
    nb = FOURIER_BATCH
    assert batch % nb == 0 and first_block % nb == 0
    return pl.pallas_call(
        _fourier_kernel,
        grid=(batch // nb, n // tn),
        in_specs=[pl.BlockSpec((tn, 2 * n), lambda b, i: (i, 0)),
                  pl.BlockSpec((nb * n, ch), lambda b, i: (first_block // nb + b, 0)),
                  pl.BlockSpec((ch, ch), lambda b, i: (0, 0)),
                  pl.BlockSpec((ch, ch), lambda b, i: (0, 0))],
        out_specs=pl.BlockSpec((nb, tn, ch), lambda b, i: (b, i, 0)),
        out_shape=jax.ShapeDtypeStruct((batch, n, ch), F32),
        scratch_shapes=[pltpu.VMEM((2 * n, nb * ch), BF16)],
        compiler_params=_cparams(("parallel", "arbitrary")),
        name="fourier_mix",
    )(cs, zf, cg, sg)


def _odd_kernel(h_ref, m_ref, z_ref, zp_ref, zn_ref, cw_ref, cb_ref, cg_ref, pw_ref, ps_ref, wu_ref, wp_ref,
                o_ref, u_scr, *, lat_tiles, lat_seq_tiles, n_lat, n_ctx, d_conv):
    i = pl.program_id(0)
    tt, d = h_ref.shape
    is_lat = i < lat_tiles
    seq_tile = jnp.where(is_lat, i % lat_seq_tiles, 0)
    seq_tiles = jnp.where(is_lat, lat_seq_tiles, n_ctx // tt)
    seq_len = jnp.where(is_lat, n_lat, n_ctx)
    zp = jnp.where(seq_tile == 0, 0.0, zp_ref[...])
    zn = jnp.where(seq_tile == seq_tiles - 1, 0.0, zn_ref[...])
    ext = jnp.concatenate([zp, z_ref[...], zn], axis=0)

    dc = d_conv
    rows = tt + 2 * HALO
    glu = ext[:, :dc] * _sigmoid(ext[:, dc:2 * dc])
    u_scr[0] = glu
    for r in range(1, 8):
        u_scr[r] = pltpu.roll(glu, rows - r, axis=0)
    acc = jnp.zeros((tt, dc), F32) + cb_ref[...]
    for k in range(CONV_WIDTH):
        shift = HALO - CONV_WIDTH // 2 + k
        acc = acc + cw_ref[k:k + 1, :] * u_scr[shift % 8, pl.ds(shift - shift % 8, tt), :]
    u = acc * lax.rsqrt(jnp.mean(acc * acc, axis=-1, keepdims=True) + NORM_EPS) * cg_ref[...]
    u = _silu(u)

    q = ext[:, 2 * dc:]
    sums, win, step = [], q, 1
    for width in POOL_WIDTHS:
        if width == 2:
            win = q + pltpu.roll(q, 1, axis=0)
        else:
            win = pltpu.roll(win, step, axis=0) + pltpu.roll(win, rows - step, axis=0)
            step *= 2
        sums.append(win[HALO:HALO + tt])
    n_groups = len(POOL_WIDTHS)
    group = lax.broadcasted_iota(jnp.int32, (tt, q.shape[1]), 1) // (q.shape[1] // n_groups)
    pos = seq_tile * tt + lax.broadcasted_iota(jnp.int32, (tt, q.shape[1]), 0)
    total = sums[-1]
    width_of = jnp.full(group.shape, POOL_WIDTHS[-1], jnp.int32)
    for gi in range(n_groups - 2, -1, -1):
        total = jnp.where(group == gi, sums[gi], total)
        width_of = jnp.where(group == gi, POOL_WIDTHS[gi], width_of)
    lo = jnp.maximum(pos - width_of // 2, 0)
    hi = jnp.minimum(pos + (width_of - 1 - width_of // 2), seq_len - 1)
    pooled = total / (hi - lo + 1).astype(F32) - q[HALO:HALO + tt]
    pooled = _dot(pooled.astype(BF16), pw_ref[...]) * ps_ref[...]

    mixed = _dot(u.astype(BF16), wu_ref[...]) + _dot(pooled.astype(BF16), wp_ref[...])
    o_ref[...] = h_ref[...] + m_ref[0, :, 5 * d:6 * d] * mixed


def _odd_mixer(tok, h, mods, z, p, *, with_ctx):
    d, d_in = tok.d, z.shape[1]
    dc = p["conv_w"].shape[1]
    dp = d_in - 2 * dc
    tt = 256
    rows = tok.rows(with_ctx)
    hb, n_hblk = tt // HALO, z.shape[0] // HALO
    full = lambda a: pl.BlockSpec(a.shape, lambda i: (0,) * a.ndim)
    consts = [p["conv_w"], p["conv_b"].reshape(1, dc), p["cnorm_g"].reshape(1, dc), p["pool_w_bd"],
              p["pool_scale"].reshape(1, dp), p["w_out_u"], p["w_out_p"]]
    return pl.pallas_call(
        functools.partial(_odd_kernel, lat_tiles=tok.t_lat // tt, lat_seq_tiles=tok.n_lat // tt,
                          n_lat=tok.n_lat, n_ctx=tok.n_ctx, d_conv=dc),
        grid=(rows // tt,),
        in_specs=[pl.BlockSpec((tt, d), lambda i: (i, 0)),
                  pl.BlockSpec((1, 1, N_MOD * d), tok.mod_index(tt)),
                  pl.BlockSpec((tt, d_in), lambda i: (i, 0)),
                  pl.BlockSpec((HALO, d_in), lambda i: (jnp.maximum(i * hb - 1, 0), 0)),
                  pl.BlockSpec((HALO, d_in), lambda i: (jnp.minimum((i + 1) * hb, n_hblk - 1), 0))]
                 + [full(a) for a in consts],
        out_specs=pl.BlockSpec((tt, d), lambda i: (i, 0)),
        out_shape=jax.ShapeDtypeStruct((rows, d), F32),
        scratch_shapes=[pltpu.VMEM((8, tt + 2 * HALO, dc), F32)],
        compiler_params=_cparams(("parallel",)),
        name="odd_mixer",
    )(h, mods, z, z, z, *consts)


def _even_mixer(tok, h, mods, norm_g, p, col_major):
    za, zf = _mix_in(tok, h, mods, norm_g, [p["w_in_a"], p["w_in_f"]], with_ctx=True)
    f_lat = _fourier_mix(zf, tok.batch, tok.n_lat, 0, col_major)
    f_ctx = _fourier_mix(zf, tok.batch, tok.n_ctx, tok.t_lat // tok.n_ctx)
    r, v, kap, lw, kd, bb, g, bonus = _rwkv_features(tok, za, p)
    y_fwd, y_bwd = _rwkv_scan(r, v, kap, lw, kd, bb, tok.n_ctx)
    return _even_out(tok, h, mods, [y_fwd, y_bwd, g, bonus], f_lat, f_ctx, p)


def _pad_rows(w, before, total):
    return jnp.pad(w, ((0, 0), (before, total - before - w.shape[1]), (0, 0)))


def _block_diag(blocks):
    g, a, b = blocks.shape
    eye = jnp.eye(g, dtype=blocks.dtype)
    return (eye[:, None, :, None] * blocks[:, :, None, :]).reshape(g * a, g * b)


def kernel(x, c, ctx, c_ctx, ada_w, ada_b, norm_g, ffn1_w_gu, ffn1_w_down, ffn2_w_gu, ffn2_w_down,
           e_w_in, e_mu, e_w0, e_w_up, e_a0, e_a_up, e_g_up, e_k_k, e_k_a, e_r_k, e_gn_w, e_gn_b, e_w_out,
           o_w_in, o_conv_w, o_conv_b, o_cnorm_g, o_pool_w, o_pool_scale, o_w_out, final_g):
    batch, n_lat, d = x.shape
    n_ctx = ctx.shape[1]
    depth = ada_w.shape[0]
    tok = _Tokens(batch, n_lat, n_ctx, d)
    assert batch < MOD_ROWS and n_lat % 512 == 0 and (batch * n_ctx) % 512 == 0 and n_ctx == FEAT_TILE
    assert n_lat % (GRID_W * 8) == 0 and FEAT_TILE % (n_lat // GRID_W) == 0 and FEAT_TILE % GRID_W == 0

    cc = jnp.zeros((MOD_ROWS, d), F32).at[:batch].set(c).at[batch].set(c_ctx)
    mods_all = _ada_table(cc, ada_w, ada_b).reshape(depth, MOD_ROWS, 1, N_MOD * d)
    h = x.reshape(batch * n_lat, d)

    dr = e_k_k.shape[-1]
    low_rank = DECAY_RANK + ICLR_RANK
    head_ones = _block_diag(jnp.ones((HEAD_SUM_LANES // HEAD, HEAD, HEAD), F32))
    n_rows = n_lat // GRID_W
    for i in range(depth):
        j = i // 2
        with_ctx = not (i == depth - 1 and i % 2 == 1)
        col_major = i % 2 == 0 and j % 2 == 1
        mods = mods_all[i]
        h = _half_ffn(tok, h, mods, norm_g[i, 0], ffn1_w_gu[i].astype(BF16), ffn1_w_down[i].astype(BF16), final_g,
                      off=0, with_ctx=with_ctx, regroup=GRID_W if col_major else None,
                      ctx_rows=ctx.reshape(batch * n_ctx, d) if i == 0 else None)
        if i % 2 == 0:
            da = e_mu.shape[-1]
            p = dict(w_in_a=e_w_in[j, :, :da].astype(BF16), w_in_f=e_w_in[j, :, da:].astype(BF16),
                     mu=e_mu[j], w0=e_w0[j], a0=e_a0[j], g_up=e_g_up[j].astype(BF16),
                     w_up_pad=_pad_rows(e_w_up[j], 0, low_rank).astype(BF16),
                     a_up_pad=_pad_rows(e_a_up[j], DECAY_RANK, low_rank).astype(BF16),
                     k_k=e_k_k[j], k_a=e_k_a[j], r_k=e_r_k[j], gn_w=e_gn_w[j], gn_b=e_gn_b[j],
                     w_out_o=e_w_out[j, :dr].astype(BF16), w_out_f=e_w_out[j, dr:].astype(BF16),
                     head_ones=head_ones.astype(BF16),
                     head_mean=(head_ones / HEAD).astype(BF16))
            h = _even_mixer(tok, h, mods, norm_g[i, 1], p, col_major)
        else:
            dc = o_conv_w.shape[-1]
            p = dict(conv_w=o_conv_w[j], conv_b=o_conv_b[j], cnorm_g=o_cnorm_g[j],
                     pool_w_bd=_block_diag(o_pool_w[j]).astype(BF16), pool_scale=o_pool_scale[j],
                     w_out_u=o_w_out[j, :dc].astype(BF16), w_out_p=o_w_out[j, dc:].astype(BF16))
            (z,) = _mix_in(tok, h, mods, norm_g[i, 1], [o_w_in[j].astype(BF16)], with_ctx=with_ctx)
            h = _odd_mixer(tok, h, mods, z, p, with_ctx=with_ctx)
        h = _half_ffn(tok, h, mods, norm_g[i, 2], ffn2_w_gu[i].astype(BF16), ffn2_w_down[i].astype(BF16), final_g,
                      off=6, with_ctx=with_ctx, final=(i == depth - 1), regroup=n_rows if col_major else None)
    return h[:tok.t_lat].reshape(batch, n_lat, d)
```

```python
import functools

import jax
import jax.numpy as jnp
import numpy as np
from jax import lax
from jax.experimental import pallas as pl
from jax.experimental.pallas import tpu as pltpu

F32 = jnp.float32
BF16 = jnp.bfloat16

N_MOD = 9
NORM_EPS = 1e-6
GRID_W = 64
HEAD = 64
PAIR = 2 * HEAD
DECAY_RANK = 64
ICLR_RANK = 64
GATE_RANK = 128
DECAY_SCALE = 0.606531
GN_EPS = 64e-5
KAP_NORM_EPS = 1e-12
HEAD_SUM_LANES = 256
FOURIER_GROUP = 64
FOURIER_BATCH = 2
CONV_WIDTH = 31
POOL_WIDTHS = (2, 4, 8, 16)
CHUNK = 64
SCAN_CHUNKS = 4
HALO = 16
FEAT_TILE = 256
MOD_ROWS = 16
VMEM_LIMIT = 56 * 1024 * 1024


def _cparams(sem):
    return pltpu.CompilerParams(dimension_semantics=sem, vmem_limit_bytes=VMEM_LIMIT)


def _dot(a, b):
    return jnp.dot(a, b, preferred_element_type=F32)


def _dot_nt(a, b):
    return lax.dot_general(a, b, (((1,), (1,)), ((), ())), preferred_element_type=F32)


def _split2(x):
    hi = x.astype(BF16)
    lo = (x - hi.astype(F32)).astype(BF16)
    return hi, lo


def _dot_x2(x, w_bf16):
    hi, lo = _split2(x)
    return _dot(hi, w_bf16) + _dot(lo, w_bf16)


def _lane_block_dot_x2(x, w_bf16):
    hi, lo = _split2(x)
    n = w_bf16.shape[0]
    return jnp.concatenate([_dot(hi[:, j:j + n], w_bf16) + _dot(lo[:, j:j + n], w_bf16)
                            for j in range(0, x.shape[1], n)], axis=1)


def _dot_x3(x, w):
    xh, xl = _split2(x)
    wh, wl = _split2(w)
    return _dot(xh, wh) + (_dot(xh, wl) + _dot(xl, wh))


def _sigmoid(x):
    return 0.5 * jnp.tanh(0.5 * x) + 0.5


def _silu(x):
    return x * _sigmoid(x)


def _modulate(h, g, shift, scale):
    y = h * lax.rsqrt(jnp.mean(h * h, axis=-1, keepdims=True) + NORM_EPS)
    return y * g * (1.0 + scale) + shift


def _ada_kernel(c_ref, w_ref, b_ref, o_ref):
    o_ref[0] = _dot_x3(_silu(c_ref[...]), w_ref[0]) + b_ref[0]


def _ada_table(cc, ada_w, ada_b):
    depth, d, nd = ada_w.shape
    tn = 1024
    return pl.pallas_call(
        _ada_kernel,
        grid=(depth, nd // tn),
        in_specs=[pl.BlockSpec((MOD_ROWS, d), lambda i, j: (0, 0)),
                  pl.BlockSpec((1, d, tn), lambda i, j: (i, 0, j)),
                  pl.BlockSpec((1, 1, tn), lambda i, j: (i, 0, j))],
        out_specs=pl.BlockSpec((1, MOD_ROWS, tn), lambda i, j: (i, 0, j)),
        out_shape=jax.ShapeDtypeStruct((depth, MOD_ROWS, nd), F32),
        compiler_params=_cparams(("parallel", "parallel")),
        name="ada_table",
    )(cc, ada_w, ada_b.reshape(depth, 1, nd))


class _Tokens:
    def __init__(self, batch, n_lat, n_ctx, d):
        self.batch, self.n_lat, self.n_ctx, self.d = batch, n_lat, n_ctx, d
        self.t_lat = batch * n_lat
        self.t_all = batch * (n_lat + n_ctx)

    def rows(self, with_ctx):
        return self.t_all if with_ctx else self.t_lat

    def mod_index(self, tm):
        lat_tiles, per_batch, ctx_row = self.t_lat // tm, self.n_lat // tm, self.batch
        return lambda t, *_: (jnp.where(t < lat_tiles, t // per_batch, ctx_row), 0, 0)


def _mod_slices(m_ref, off, d):
    return [m_ref[0, :, (off + k) * d:(off + k + 1) * d] for k in range(3)]


def _resident(shape):
    return pl.BlockSpec(shape, lambda *_: (0,) * len(shape), pipeline_mode=pl.Buffered(1))


def _ffn_kernel(*refs, off, final, lat_tiles, alt):
    alt_ref = refs[0] if alt else None
    h_ref, m_ref, g_ref, wgu_ref, wd_ref, fg_ref, o_ref = refs[-7:]
    d, ff = h_ref.shape[1], wd_ref.shape[0]
    shift, scale, gate = _mod_slices(m_ref, off, d)
    h = h_ref[...]
    is_lat = pl.program_id(0) < lat_tiles
    if alt == "regroup":
        h = jnp.where(is_lat, jnp.concatenate([alt_ref[:, c, :] for c in range(alt_ref.shape[1])], axis=0), h)
    elif alt == "ctx":
        h = jnp.where(is_lat, h, alt_ref[...])
    half = h.shape[0] // 2
    for part in (slice(0, half), slice(half, 2 * half)):
        hp = h[part]
        y = _modulate(hp, g_ref[...], shift, scale).astype(BF16)
        gu = _dot(y, wgu_ref[...])
        act = (_silu(gu[:, :ff]) * gu[:, ff:]).astype(BF16)
        out = hp + (0.5 * gate) * _dot(act, wd_ref[...])
        if final:
            out = out * lax.rsqrt(jnp.mean(out * out, axis=-1, keepdims=True) + NORM_EPS) * fg_ref[...]
        o_ref[part, :] = out


def _half_ffn(tok, h, mods, g, w_gu, w_down, final_g, *, off, with_ctx, final=False, regroup=None, ctx_rows=None):
    d = tok.d
    tm = 512
    rows = tok.rows(with_ctx)
    lat_tiles = tok.t_lat // tm
    lat_only = lambda t: (jnp.minimum(t, lat_tiles - 1), 0)
    ctx_only = lambda t: (jnp.maximum(t - lat_tiles, 0), 0)
    ins, specs, alt = [], [], None
    h_spec = pl.BlockSpec((tm, d), lambda t: (t, 0))
    if regroup is not None:
        alt = "regroup"
        groups = tm // (tok.n_lat // regroup)
        per_batch = regroup // groups
        ins.append(h.reshape(h.shape[0] // regroup, regroup, d))
        specs.append(pl.BlockSpec((tm // groups, groups, d),
                                  lambda t: (lat_only(t)[0] // per_batch, lat_only(t)[0] % per_batch, 0)))
        h_spec = pl.BlockSpec((tm, d), lambda t: (jnp.maximum(t, lat_tiles), 0))
    elif ctx_rows is not None:
        alt = "ctx"
        ins.append(ctx_rows)
        specs.append(pl.BlockSpec((tm, d), ctx_only))
        h_spec = pl.BlockSpec((tm, d), lat_only)
    return pl.pallas_call(
        functools.partial(_ffn_kernel, off=off, final=final, lat_tiles=lat_tiles, alt=alt),
        grid=(rows // tm,),
        in_specs=specs + [h_spec,
                          pl.BlockSpec((1, 1, N_MOD * d), tok.mod_index(tm)),
                          pl.BlockSpec((1, d), lambda t: (0, 0)),
                          _resident(w_gu.shape),
                          _resident(w_down.shape),
                          pl.BlockSpec((1, d), lambda t: (0, 0))],
        out_specs=pl.BlockSpec((tm, d), lambda t: (t, 0)),
        out_shape=jax.ShapeDtypeStruct((rows, d), F32),
        compiler_params=_cparams(("parallel",)),
        name="half_ffn",
    )(*ins, h, mods, g.reshape(1, d), w_gu, w_down, final_g.reshape(1, d))


def _mix_in_kernel(h_ref, m_ref, g_ref, *refs):
    d = h_ref.shape[1]
    w_refs, z_refs = refs[:len(refs) // 2], refs[len(refs) // 2:]
    shift, scale, _ = _mod_slices(m_ref, 3, d)
    y = _modulate(h_ref[...], g_ref[...], shift, scale).astype(BF16)
    for w_ref, z_ref in zip(w_refs, z_refs):
        z_ref[...] = _dot(y, w_ref[...]).astype(z_ref.dtype)


def _mix_in(tok, h, mods, g, weights, out_dtypes, *, with_ctx):
    d = tok.d
    tm = 512
    rows = tok.rows(with_ctx)
    return pl.pallas_call(
        _mix_in_kernel,
        grid=(rows // tm,),
        in_specs=[pl.BlockSpec((tm, d), lambda t: (t, 0)),
                  pl.BlockSpec((1, 1, N_MOD * d), tok.mod_index(tm)),
                  pl.BlockSpec((1, d), lambda t: (0, 0))]
                 + [_resident(w.shape) for w in weights],
        out_specs=[pl.BlockSpec((tm, w.shape[1]), lambda t: (t, 0)) for w in weights],
        out_shape=[jax.ShapeDtypeStruct((rows, w.shape[1]), dt) for w, dt in zip(weights, out_dtypes)],
        compiler_params=_cparams(("parallel",)),
        name="mix_in",
    )(h, mods, g.reshape(1, d), *weights)


def _feat_kernel(z_ref, zp_ref, zn_ref, mu_ref, wup_ref, aup_ref, gup_ref, w0_ref, a0_ref, kk_ref, ka_ref, rk_ref,
                 ones_ref, r_out, v_out, kap_out, lw_out, kd_out, b_out, g_out, bonus_out, *, lat_tiles, d_rwkv):
    i = pl.program_id(1)
    tt = z_ref.shape[0]
    is_ctx = i == lat_tiles
    z = z_ref[...].astype(F32)
    first_of_seq = jnp.logical_or(i == 0, is_ctx)
    last_of_seq = jnp.logical_or(i == lat_tiles - 1, is_ctx)
    halo = zp_ref.shape[0]
    prev_row = jnp.where(first_of_seq, 0.0, zp_ref[halo - 1:halo, :].astype(F32))
    next_row = jnp.where(last_of_seq, 0.0, zn_ref[0:1, :].astype(F32))
    row = lax.broadcasted_iota(jnp.int32, (tt, 1), 0)
    z_prev = jnp.where(row == 0, prev_row, pltpu.roll(z, 1, axis=0))
    z_next = jnp.where(row == tt - 1, next_row, pltpu.roll(z, tt - 1, axis=0))
    z = z + mu_ref[...] * (0.5 * (z_prev + z_next) - z)

    dr = d_rwkv
    r, k, v = z[:, :dr], z[:, dr:2 * dr], z[:, 2 * dr:3 * dr]
    low = z[:, 3 * dr:3 * dr + DECAY_RANK + ICLR_RANK]
    gd = z[:, 3 * dr + DECAY_RANK + ICLR_RANK:]
    ones = ones_ref[...]

    kap = k * kk_ref[...]
    kap = kap * lax.rsqrt(jnp.maximum(_lane_block_dot_x2(kap * kap, ones), KAP_NORM_EPS * KAP_NORM_EPS))
    tanh_low = jnp.tanh(low)
    low_bf16 = low.astype(BF16)
    kd_sum = jnp.zeros_like(k)
    for dd in range(2):
        lw_out[dd, 0] = -DECAY_SCALE * _sigmoid(w0_ref[dd] + _dot_x2(tanh_low, wup_ref[dd]))
        a = _sigmoid(a0_ref[dd] + _dot(low_bf16, aup_ref[dd]))
        kd = k * (1.0 + (a - 1.0) * ka_ref[...])
        kd_out[dd, 0] = kd.astype(kd_out.dtype)
        b_out[dd, 0] = (kap * a).astype(b_out.dtype)
        kd_sum = kd_sum + kd
    r_out[0] = r.astype(r_out.dtype)
    v_out[0] = v.astype(v_out.dtype)
    kap_out[0] = kap.astype(kap_out.dtype)
    g_out[0] = _dot(_sigmoid(gd).astype(BF16), gup_ref[...]).astype(g_out.dtype)
    bonus_out[0] = (_lane_block_dot_x2(r * kd_sum * rk_ref[...], ones) * v).astype(bonus_out.dtype)


def _rwkv_features(tok, za, p):
    batch, n, nc = tok.batch, tok.n_lat, tok.n_ctx
    t = n + nc
    da = za.shape[1]
    dr = p["k_k"].shape[-1]
    tt = FEAT_TILE
    lat_tiles = n // tt
    ctx_block = tok.t_lat // tt
    halo = 16
    hb = tt // halo
    n_hblk = za.shape[0] // halo
    row = lambda name: p[name].reshape(1, dr)
    tile = lambda b, i: jnp.where(i < lat_tiles, b * lat_tiles + i, ctx_block + b)
    seq_spec = pl.BlockSpec((1, tt, dr), lambda b, i: (b, i, 0))
    dir_spec = pl.BlockSpec((2, 1, tt, dr), lambda b, i: (0, b, i, 0))
    full = lambda a: pl.BlockSpec(a.shape, lambda b, i: (0,) * a.ndim)
    consts = [p["mu"].reshape(1, da), p["w_up_pad"], p["a_up_pad"], p["g_up"],
              p["w0"].reshape(2, 1, dr), p["a0"].reshape(2, 1, dr), row("k_k"), row("k_a"), row("r_k"),
              p["head_ones"]]
    seq = lambda dt: jax.ShapeDtypeStruct((batch, t, dr), dt)
    dirs = lambda dt: jax.ShapeDtypeStruct((2, batch, t, dr), dt)
    return pl.pallas_call(
        functools.partial(_feat_kernel, lat_tiles=lat_tiles, d_rwkv=dr),
        grid=(batch, t // tt),
        in_specs=[pl.BlockSpec((tt, da), lambda b, i: (tile(b, i), 0)),
                  pl.BlockSpec((halo, da), lambda b, i: (jnp.maximum(tile(b, i) * hb - 1, 0), 0)),
                  pl.BlockSpec((halo, da), lambda b, i: (jnp.minimum((tile(b, i) + 1) * hb, n_hblk - 1), 0))]
                 + [full(a) for a in consts],
        out_specs=[seq_spec, seq_spec, seq_spec, dir_spec, dir_spec, dir_spec, seq_spec, seq_spec],
        out_shape=[seq(BF16), seq(BF16), seq(BF16), dirs(F32), dirs(BF16), dirs(BF16), seq(BF16), seq(BF16)],
        compiler_params=_cparams(("parallel", "parallel")),
        name="rwkv_features",
    )(za, za, za, *consts)


def _chunk_masks(ch, rev):
    ri = lax.broadcasted_iota(jnp.int32, (PAIR, PAIR), 0)
    ci = lax.broadcasted_iota(jnp.int32, (PAIR, PAIR), 1)
    same_head = (ri // ch) == (ci // ch)
    rt, ct = ri % ch, ci % ch
    before, upto = (ct > rt, ct >= rt) if rev else (ct < rt, ct <= rt)
    return dict(same_head=same_head, diag=ri == ci,
                strict=jnp.logical_and(same_head, before), incl=jnp.logical_and(same_head, upto))


def _scan_kernel(rf_ref, vf_ref, kapf_ref, lwf_ref, kdf_ref, bf_ref, rb_ref, vb_ref, kapb_ref, lwb_ref, kdb_ref,
                 bb_ref, yf_ref, yb_ref, s_scr):
    c = pl.program_id(1)
    ch = CHUNK
    n_sub = rf_ref.shape[1] // ch
    n_pairs = rf_ref.shape[2] // PAIR

    @pl.when(c == 0)
    def _():
        s_scr[...] = jnp.zeros_like(s_scr)

    row_id = lax.broadcasted_iota(jnp.int32, (ch, 1), 0)
    lane_head = lax.broadcasted_iota(jnp.int32, (ch, PAIR), 1) // HEAD
    head0, head1 = lane_head == 0, lane_head == 1

    def stack_heads(x):
        return jnp.concatenate([jnp.where(head0, x, 0.0), jnp.where(head1, x, 0.0)], axis=0)

    def fold_heads(x):
        return x[:ch] + x[ch:]

    chains = []
    dirs = ((rf_ref, vf_ref, kapf_ref, lwf_ref, kdf_ref, bf_ref, yf_ref),
            (rb_ref, vb_ref, kapb_ref, lwb_ref, kdb_ref, bb_ref, yb_ref))
    for d, (r_ref, v_ref, kap_ref, lw_ref, kd_ref, b_ref, y_ref) in enumerate(dirs):
        mk = _chunk_masks(ch, rev=(d == 1))
        for sub in (range(n_sub) if d == 0 else reversed(range(n_sub))):
            rows = slice(sub * ch, (sub + 1) * ch)
            lw_all = lw_ref[0, 0, rows, :]
            cl_all = lw_all
            for step in (1 << k for k in range(int(np.log2(ch)))):
                if d == 0:
                    cl_all = cl_all + jnp.where(row_id >= step, pltpu.roll(cl_all, step, axis=0), 0.0)
                else:
                    cl_all = cl_all + jnp.where(row_id < ch - step, pltpu.roll(cl_all, ch - step, axis=0), 0.0)
            for p in range(n_pairs):
                sl = slice(p * PAIR, (p + 1) * PAIR)
                f32 = lambda ref, *lead: ref[(*lead, rows, sl)].astype(F32)
                chains.append(dict(d=d, p=p, sl=sl, rows=rows, mk=mk, y_ref=y_ref, lw=lw_all[:, sl], cl=cl_all[:, sl],
                                   r=f32(r_ref, 0), v=f32(v_ref, 0), kap=f32(kap_ref, 0),
                                   kd=f32(kd_ref, 0, 0), b=f32(b_ref, 0, 0)))

    for q in chains:
        lw, cl, mk = q["lw"], q["cl"], q["mk"]
        tot = jnp.sum(lw, axis=0, keepdims=True)
        g_inv, g_rem = jnp.exp(-cl), jnp.exp(tot - cl)
        kq = stack_heads(q["kap"] * jnp.exp(cl - lw))
        rq = stack_heads(q["r"] * jnp.exp(cl))
        kdd, bdd = q["kd"] * g_inv, q["b"] * g_inv
        gram = _dot_nt(jnp.concatenate([kq, rq], axis=0).astype(BF16),
                       jnp.concatenate([bdd, bdd, kdd, kdd], axis=0).astype(BF16))
        q.update(kq=kq, rq=rq, g_tot=jnp.exp(tot),
                 kb=jnp.concatenate([q["kd"] * g_rem, q["b"] * g_rem], axis=0).astype(BF16),
                 l_b=jnp.where(mk["strict"], gram[:PAIR, :PAIR], 0.0),
                 l_k=jnp.where(mk["strict"], gram[:PAIR, PAIR:], 0.0),
                 pkb=jnp.concatenate([jnp.where(mk["incl"], gram[PAIR:, PAIR:], 0.0),
                                      jnp.where(mk["incl"], -gram[PAIR:, :PAIR], 0.0)], axis=1).astype(BF16))
    for q in chains:
        q["v2"] = jnp.concatenate([q["v"], q["v"]], axis=0).astype(BF16)
        lkv = jnp.where(q["mk"]["same_head"], _dot(q["l_k"].astype(BF16), q["v2"]), 0.0)
        q["rhs"] = jnp.concatenate([q["kq"], lkv], axis=1).astype(BF16)
        q["lb"] = q["l_b"].astype(BF16)
    for q in chains:
        q["lp"] = _dot(q["lb"], q["lb"]).astype(BF16)
        q["t"] = jnp.where(q["mk"]["diag"], 1.0, 0.0) - q["l_b"]
    n_sq = int(np.log2(ch))
    for k in range(1, n_sq):
        for q in chains:
            w = jnp.concatenate([q["t"].astype(BF16), q["lp"]], axis=1) if k < n_sq - 1 else q["t"].astype(BF16)
            q["both"] = _dot(q["lp"], w)
        for q in chains:
            q["t"] = q["t"] + q["both"][:, :PAIR]
            if k < n_sq - 1:
                q["lp"] = q["both"][:, PAIR:].astype(BF16)
    for q in chains:
        q["x"] = _dot(q["t"].astype(BF16), q["rhs"])
    for q in chains:
        x = q["x"]
        mk = q["mk"]
        xk, w1 = x[:, :PAIR], x[:, PAIR:]
        rhs = jnp.concatenate([jnp.concatenate([q["v2"], jnp.zeros_like(q["v2"])], axis=1),
                               jnp.concatenate([w1, xk], axis=1).astype(BF16)], axis=0)
        big = _dot(q["pkb"], rhs)
        q["y0"] = fold_heads(jnp.where(mk["same_head"], big[:, :PAIR], 0.0))
        q["r_eff"] = (q["rq"] + big[:, PAIR:]).astype(BF16)
        lhs = jnp.concatenate([jnp.concatenate([q["v"], -fold_heads(w1)], axis=0),
                               jnp.concatenate([jnp.zeros_like(q["v"]), -fold_heads(xk)], axis=0)], axis=1)
        upd = _dot(lhs.T.astype(BF16), q["kb"])
        q["z0"] = jnp.where(mk["same_head"], upd[:PAIR], 0.0)
        q["a"] = jnp.where(mk["diag"], q["g_tot"], jnp.where(mk["same_head"], upd[PAIR:], 0.0)).astype(BF16)
    for q in chains:
        s0 = s_scr[q["d"], q["p"]].astype(BF16)
        q["y_ref"][0, q["rows"], q["sl"]] = (q["y0"] + fold_heads(_dot_nt(q["r_eff"], s0))).astype(BF16)
        s_scr[q["d"], q["p"]] = _dot(s0, q["a"]) + q["z0"]


def _rwkv_scan(r, v, kap, lw, kd, b, n_ctx):
    batch, t, dr = r.shape
    rows = SCAN_CHUNKS * CHUNK
    assert t % rows == 0 and n_ctx % rows == 0
    n_blocks, ctx_blocks = t // rows, n_ctx // rows
    lat_blocks = n_blocks - ctx_blocks

    def fwd_block(c):
        return jnp.where(c < ctx_blocks, lat_blocks + c, c - ctx_blocks)

    def bwd_block(c):
        return n_blocks - 1 - c

    fwd_seq = pl.BlockSpec((1, rows, dr), lambda bb, c: (bb, fwd_block(c), 0))
    bwd_seq = pl.BlockSpec((1, rows, dr), lambda bb, c: (bb, bwd_block(c), 0))
    fwd_dir = pl.BlockSpec((1, 1, rows, dr), lambda bb, c: (0, bb, fwd_block(c), 0))
    bwd_dir = pl.BlockSpec((1, 1, rows, dr), lambda bb, c: (1, bb, bwd_block(c), 0))
    y_shape = jax.ShapeDtypeStruct((batch, t, dr), BF16)
    return pl.pallas_call(
        _scan_kernel,
        grid=(batch, n_blocks),
        in_specs=[fwd_seq, fwd_seq, fwd_seq, fwd_dir, fwd_dir, fwd_dir,
                  bwd_seq, bwd_seq, bwd_seq, bwd_dir, bwd_dir, bwd_dir],
        out_specs=[fwd_seq, bwd_seq],
        out_shape=[y_shape, y_shape],
        scratch_shapes=[pltpu.VMEM((2, dr // PAIR, PAIR, PAIR), F32)],
        compiler_params=_cparams(("parallel", "arbitrary")),
        name="rwkv_scan",
    )(r, v, kap, lw, kd, b, r, v, kap, lw, kd, b)


def _even_out_kernel(h_ref, m_ref, yf_ref, yb_ref, g_ref, bonus_ref, fl_ref, fc_ref, gnw_ref, gnb_ref, mean_ref,
                     wo_ref, wf_ref, o_ref, *, lat_tiles):
    d = h_ref.shape[1]
    y = yf_ref[0].astype(F32) + yb_ref[0].astype(F32)
    mean_mat = mean_ref[...]
    dev = y - _lane_block_dot_x2(y, mean_mat)
    var = _lane_block_dot_x2(dev * dev, mean_mat)
    yn = dev * lax.rsqrt(var + GN_EPS) * gnw_ref[...] + gnb_ref[...]
    o = ((yn + bonus_ref[0].astype(F32)) * g_ref[0].astype(F32)).astype(BF16)
    f = jnp.where(pl.program_id(0) < lat_tiles, fl_ref[...], fc_ref[...]).astype(BF16)
    mixed = _dot(o, wo_ref[...]) + _dot(f, wf_ref[...])
    o_ref[...] = h_ref[...] + m_ref[0, :, 5 * d:6 * d] * mixed


def _even_out(tok, h, mods, seq_arrays, f_lat, f_ctx, p):
    d, batch, n = tok.d, tok.batch, tok.n_lat
    dr = p["gn_w"].shape[-1]
    tt = FEAT_TILE
    per_batch = n // tt
    lat_tiles = batch * per_batch
    seq_spec = pl.BlockSpec((1, tt, dr), lambda t: (jnp.where(t < lat_tiles, t // per_batch, t - lat_tiles),
                                                    jnp.where(t < lat_tiles, t % per_batch, per_batch), 0))
    df = f_lat.shape[-1]
    vec = lambda a: pl.BlockSpec((1, a.shape[-1]), lambda t: (0, 0))
    return pl.pallas_call(
        functools.partial(_even_out_kernel, lat_tiles=lat_tiles),
        grid=(tok.t_all // tt,),
        in_specs=[pl.BlockSpec((tt, d), lambda t: (t, 0)),
                  pl.BlockSpec((1, 1, N_MOD * d), tok.mod_index(tt))]
                 + [seq_spec] * len(seq_arrays)
                 + [pl.BlockSpec((tt, df), lambda t: (jnp.minimum(t, lat_tiles - 1), 0)),
                    pl.BlockSpec((tt, df), lambda t: (jnp.maximum(t - lat_tiles, 0), 0)),
                    vec(p["gn_w"]), vec(p["gn_b"]), _resident(p["head_mean"].shape),
                    _resident(p["w_out_o"].shape), _resident(p["w_out_f"].shape)],
        out_specs=pl.BlockSpec((tt, d), lambda t: (t, 0)),
        out_shape=jax.ShapeDtypeStruct((tok.t_all, d), F32),
        compiler_params=_cparams(("parallel",)),
        name="even_out",
    )(h, mods, *seq_arrays, f_lat.reshape(-1, df), f_ctx.reshape(-1, df), p["gn_w"].reshape(1, dr),
      p["gn_b"].reshape(1, dr), p["head_mean"], p["w_out_o"], p["w_out_f"])


def _fourier_kernel(cs_ref, u_ref, cg_ref, sg_ref, o_ref, stacked_scr):
    n, ch = u_ref.shape[0] // FOURIER_BATCH, u_ref.shape[1]

    @pl.when(pl.program_id(1) == 0)
    def _():
        for k in range(FOURIER_BATCH):
            u_hi, u_lo = _split2(u_ref[k * n:(k + 1) * n, :])
            cols = slice(k * ch, (k + 1) * ch)
            stacked_scr[:n, cols] = (_dot(u_hi, cg_ref[...]) + _dot(u_lo, cg_ref[...])).astype(BF16)
            stacked_scr[n:, cols] = (_dot(u_hi, sg_ref[...]) + _dot(u_lo, sg_ref[...])).astype(BF16)

    out = _dot(cs_ref[...], stacked_scr[...])
    for k in range(FOURIER_BATCH):
        o_ref[k] = out[:, k * ch:(k + 1) * ch]


def _dft_tables(n, groups, col_major):
    n_hi = n // GRID_W
    idx = jnp.arange(n, dtype=jnp.int32)
    pos = (idx % n_hi) * GRID_W + idx // n_hi if col_major else idx
    k = pos[:, None]
    ang_hi = ((k * (jnp.arange(n_hi, dtype=jnp.int32) * GRID_W)[None, :]) % n).astype(F32) * (2.0 * np.pi / n)
    ang_lo = ((k * jnp.arange(GRID_W, dtype=jnp.int32)[None, :]) % n).astype(F32) * (2.0 * np.pi / n)
    scale = 1.0 / np.sqrt(n * FOURIER_GROUP)
    c_hi, s_hi = jnp.cos(ang_hi) * scale, jnp.sin(ang_hi) * scale
    c_lo, s_lo = jnp.cos(ang_lo), jnp.sin(ang_lo)
    if col_major:
        hi, lo = (lambda a: a[:, None, :]), (lambda a: a[:, :, None])
    else:
        hi, lo = (lambda a: a[:, :, None]), (lambda a: a[:, None, :])
    cos = (hi(c_hi) * lo(c_lo) - hi(s_hi) * lo(s_lo)).reshape(n, n)
    sin = (hi(s_hi) * lo(c_lo) + hi(c_hi) * lo(s_lo)).reshape(n, n)
    cs = jnp.concatenate([cos, -sin], axis=1).astype(BF16)
    kg = np.arange(FOURIER_GROUP)
    ang_g = 2.0 * np.pi * ((kg[:, None] * kg[None, :]) % FOURIER_GROUP) / FOURIER_GROUP
    eye = np.eye(groups)
    cg = jnp.asarray(np.kron(eye, np.cos(ang_g)), F32).astype(BF16)
    sg = jnp.asarray(np.kron(eye, np.sin(ang_g)), F32).astype(BF16)
    return cs, cg, sg


def _fourier_mix(zf, batch, n, first_block, col_major=False):
    ch = zf.shape[1]
    cs, cg, sg = _dft_tables(n, ch // FOURIER_GROUP, col_major)
    tn = min(n, 512)
    nb = FOURIER_BATCH
    assert batch % nb == 0 and first_block % nb == 0
    return pl.pallas_call(
        _fourier_kernel,
        grid=(batch // nb, n // tn),
        in_specs=[pl.BlockSpec((tn, 2 * n), lambda b, i: (i, 0)),
                  pl.BlockSpec((nb * n, ch), lambda b, i: (first_block // nb + b, 0)),
                  pl.BlockSpec((ch, ch), lambda b, i: (0, 0)),
                  pl.BlockSpec((ch, ch), lambda b, i: (0, 0))],
        out_specs=pl.BlockSpec((nb, tn, ch), lambda b, i: (b, i, 0)),
        out_shape=jax.ShapeDtypeStruct((batch, n, ch), F32),
        scratch_shapes=[pltpu.VMEM((2 * n, nb * ch), BF16)],
        compiler_params=_cparams(("parallel", "arbitrary")),
        name="fourier_mix",
    )(cs, zf, cg, sg)


def _odd_kernel(h_ref, m_ref, z_ref, zp_ref, zn_ref, cw_ref, cb_ref, cg_ref, pw_ref, ps_ref, wu_ref, wp_ref,
                o_ref, u_scr, *, lat_tiles, lat_seq_tiles, n_lat, n_ctx, d_conv):
    i = pl.program_id(0)
    tt, d = h_ref.shape
    is_lat = i < lat_tiles
    seq_tile = jnp.where(is_lat, i % lat_seq_tiles, 0)
    seq_tiles = jnp.where(is_lat, lat_seq_tiles, n_ctx // tt)
    seq_len = jnp.where(is_lat, n_lat, n_ctx)
    zp = jnp.where(seq_tile == 0, 0.0, zp_ref[...])
    zn = jnp.where(seq_tile == seq_tiles - 1, 0.0, zn_ref[...])
    ext = jnp.concatenate([zp, z_ref[...], zn], axis=0)

    dc = d_conv
    rows = tt + 2 * HALO
    glu = ext[:, :dc] * _sigmoid(ext[:, dc:2 * dc])
    u_scr[0] = glu
    for r in range(1, 8):
        u_scr[r] = pltpu.roll(glu, rows - r, axis=0)
    acc = jnp.zeros((tt, dc), F32) + cb_ref[...]
    for k in range(CONV_WIDTH):
        shift = HALO - CONV_WIDTH // 2 + k
        acc = acc + cw_ref[k:k + 1, :] * u_scr[shift % 8, pl.ds(shift - shift % 8, tt), :]
    u = acc * lax.rsqrt(jnp.mean(acc * acc, axis=-1, keepdims=True) + NORM_EPS) * cg_ref[...]
    u = _silu(u)

    q = ext[:, 2 * dc:]
    sums, win, step = [], q, 1
    for width in POOL_WIDTHS:
        if width == 2:
            win = q + pltpu.roll(q, 1, axis=0)
        else:
            win = pltpu.roll(win, step, axis=0) + pltpu.roll(win, rows - step, axis=0)
            step *= 2
        sums.append(win[HALO:HALO + tt])
    n_groups = len(POOL_WIDTHS)
    group = lax.broadcasted_iota(jnp.int32, (tt, q.shape[1]), 1) // (q.shape[1] // n_groups)
    pos = seq_tile * tt + lax.broadcasted_iota(jnp.int32, (tt, q.shape[1]), 0)
    total = sums[-1]
    width_of = jnp.full(group.shape, POOL_WIDTHS[-1], jnp.int32)
    for gi in range(n_groups - 2, -1, -1):
        total = jnp.where(group == gi, sums[gi], total)
        width_of = jnp.where(group == gi, POOL_WIDTHS[gi], width_of)
    lo = jnp.maximum(pos - width_of // 2, 0)
    hi = jnp.minimum(pos + (width_of - 1 - width_of // 2), seq_len - 1)
    pooled = total / (hi - lo + 1).astype(F32) - q[HALO:HALO + tt]
    pooled = _dot(pooled.astype(BF16), pw_ref[...]) * ps_ref[...]

    mixed = _dot(u.astype(BF16), wu_ref[...]) + _dot(pooled.astype(BF16), wp_ref[...])
    o_ref[...] = h_ref[...] + m_ref[0, :, 5 * d:6 * d] * mixed


def _odd_mixer(tok, h, mods, z, p, *, with_ctx):
    d, d_in = tok.d, z.shape[1]
    dc = p["conv_w"].shape[1]
    dp = d_in - 2 * dc
    tt = 256
    rows = tok.rows(with_ctx)
    hb, n_hblk = tt // HALO, z.shape[0] // HALO
    full = lambda a: pl.BlockSpec(a.shape, lambda i: (0,) * a.ndim)
    consts = [p["conv_w"], p["conv_b"].reshape(1, dc), p["cnorm_g"].reshape(1, dc), p["pool_w_bd"],
              p["pool_scale"].reshape(1, dp), p["w_out_u"], p["w_out_p"]]
    return pl.pallas_call(
        functools.partial(_odd_kernel, lat_tiles=tok.t_lat // tt, lat_seq_tiles=tok.n_lat // tt,
                          n_lat=tok.n_lat, n_ctx=tok.n_ctx, d_conv=dc),
        grid=(rows // tt,),
        in_specs=[pl.BlockSpec((tt, d), lambda i: (i, 0)),
                  pl.BlockSpec((1, 1, N_MOD * d), tok.mod_index(tt)),
                  pl.BlockSpec((tt, d_in), lambda i: (i, 0)),
                  pl.BlockSpec((HALO, d_in), lambda i: (jnp.maximum(i * hb - 1, 0), 0)),
                  pl.BlockSpec((HALO, d_in), lambda i: (jnp.minimum((i + 1) * hb, n_hblk - 1), 0))]
                 + [full(a) for a in consts],
        out_specs=pl.BlockSpec((tt, d), lambda i: (i, 0)),
        out_shape=jax.ShapeDtypeStruct((rows, d), F32),
        scratch_shapes=[pltpu.VMEM((8, tt + 2 * HALO, dc), F32)],
        compiler_params=_cparams(("parallel",)),
        name="odd_mixer",
    )(h, mods, z, z, z, *consts)


def _even_mixer(tok, h, mods, norm_g, p, col_major):
    za, zf = _mix_in(tok, h, mods, norm_g, [p["w_in_a"], p["w_in_f"]], [BF16, F32], with_ctx=True)
    f_lat = _fourier_mix(zf, tok.batch, tok.n_lat, 0, col_major)
    f_ctx = _fourier_mix(zf, tok.batch, tok.n_ctx, tok.t_lat // tok.n_ctx)
    r, v, kap, lw, kd, bb, g, bonus = _rwkv_features(tok, za, p)
    y_fwd, y_bwd = _rwkv_scan(r, v, kap, lw, kd, bb, tok.n_ctx)
    return _even_out(tok, h, mods, [y_fwd, y_bwd, g, bonus], f_lat, f_ctx, p)


def _pad_rows(w, before, total):
    return jnp.pad(w, ((0, 0), (before, total - before - w.shape[1]), (0, 0)))


def _block_diag(blocks):
    g, a, b = blocks.shape
    eye = jnp.eye(g, dtype=blocks.dtype)
    return (eye[:, None, :, None] * blocks[:, :, None, :]).reshape(g * a, g * b)


def kernel(x, c, ctx, c_ctx, ada_w, ada_b, norm_g, ffn1_w_gu, ffn1_w_down, ffn2_w_gu, ffn2_w_down,
           e_w_in, e_mu, e_w0, e_w_up, e_a0, e_a_up, e_g_up, e_k_k, e_k_a, e_r_k, e_gn_w, e_gn_b, e_w_out,
           o_w_in, o_conv_w, o_conv_b, o_cnorm_g, o_pool_w, o_pool_scale, o_w_out, final_g):
    batch, n_lat, d = x.shape
    n_ctx = ctx.shape[1]
    depth = ada_w.shape[0]
    tok = _Tokens(batch, n_lat, n_ctx, d)
    assert batch < MOD_ROWS and n_lat % 512 == 0 and (batch * n_ctx) % 512 == 0 and n_ctx == FEAT_TILE
    assert n_lat % (GRID_W * 8) == 0 and FEAT_TILE % (n_lat // GRID_W) == 0 and FEAT_TILE % GRID_W == 0

    cc = jnp.zeros((MOD_ROWS, d), F32).at[:batch].set(c).at[batch].set(c_ctx)
    mods_all = _ada_table(cc, ada_w, ada_b).reshape(depth, MOD_ROWS, 1, N_MOD * d)
    h = x.reshape(batch * n_lat, d)

    dr = e_k_k.shape[-1]
    low_rank = DECAY_RANK + ICLR_RANK
    head_ones = _block_diag(jnp.ones((HEAD_SUM_LANES // HEAD, HEAD, HEAD), F32))
    n_rows = n_lat // GRID_W
    for i in range(depth):
        j = i // 2
        with_ctx = not (i == depth - 1 and i % 2 == 1)
        col_major = i % 2 == 0 and j % 2 == 1
        mods = mods_all[i]
        h = _half_ffn(tok, h, mods, norm_g[i, 0], ffn1_w_gu[i].astype(BF16), ffn1_w_down[i].astype(BF16), final_g,
                      off=0, with_ctx=with_ctx, regroup=GRID_W if col_major else None,
                      ctx_rows=ctx.reshape(batch * n_ctx, d) if i == 0 else None)
        if i % 2 == 0:
            da = e_mu.shape[-1]
            p = dict(w_in_a=e_w_in[j, :, :da].astype(BF16), w_in_f=e_w_in[j, :, da:].astype(BF16),
                     mu=e_mu[j], w0=e_w0[j], a0=e_a0[j], g_up=e_g_up[j].astype(BF16),
                     w_up_pad=_pad_rows(e_w_up[j], 0, low_rank).astype(BF16),
                     a_up_pad=_pad_rows(e_a_up[j], DECAY_RANK, low_rank).astype(BF16),
                     k_k=e_k_k[j], k_a=e_k_a[j], r_k=e_r_k[j], gn_w=e_gn_w[j], gn_b=e_gn_b[j],
                     w_out_o=e_w_out[j, :dr].astype(BF16), w_out_f=e_w_out[j, dr:].astype(BF16),
                     head_ones=head_ones.astype(BF16),
                     head_mean=(head_ones / HEAD).astype(BF16))
            h = _even_mixer(tok, h, mods, norm_g[i, 1], p, col_major)
        else:
            dc = o_conv_w.shape[-1]
            p = dict(conv_w=o_conv_w[j], conv_b=o_conv_b[j], cnorm_g=o_cnorm_g[j],
                     pool_w_bd=_block_diag(o_pool_w[j]).astype(BF16), pool_scale=o_pool_scale[j],
                     w_out_u=o_w_out[j, :dc].astype(BF16), w_out_p=o_w_out[j, dc:].astype(BF16))
            (z,) = _mix_in(tok, h, mods, norm_g[i, 1], [o_w_in[j].astype(BF16)], [F32], with_ctx=with_ctx)
            h = _odd_mixer(tok, h, mods, z, p, with_ctx=with_ctx)
        h = _half_ffn(tok, h, mods, norm_g[i, 2], ffn2_w_gu[i].astype(BF16), ffn2_w_down[i].astype(BF16), final_g,
                      off=6, with_ctx=with_ctx, final=(i == depth - 1), regroup=n_rows if col_major else None)
    return h[:tok.t_lat].reshape(batch, n_lat, d)
```

```python
import functools

import jax
import jax.numpy as jnp
import numpy as np
from jax import lax
from jax.experimental import pallas as pl
from jax.experimental.pallas import tpu as pltpu

F32 = jnp.float32
BF16 = jnp.bfloat16

N_MOD = 9
NORM_EPS = 1e-6
GRID_W = 64
HEAD = 64
PAIR = 2 * HEAD
DECAY_RANK = 64
ICLR_RANK = 64
GATE_RANK = 128
DECAY_SCALE = 0.606531
GN_EPS = 64e-5
KAP_NORM_EPS = 1e-12
HEAD_SUM_LANES = 256
FOURIER_GROUP = 64
FOURIER_BATCH = 2
CONV_WIDTH = 31
POOL_WIDTHS = (2, 4, 8, 16)
CHUNK = 64
SCAN_CHUNKS = 4
HALO = 16
FEAT_TILE = 256
MOD_ROWS = 16
VMEM_LIMIT = 56 * 1024 * 1024


def _cparams(sem):
    return pltpu.CompilerParams(dimension_semantics=sem, vmem_limit_bytes=VMEM_LIMIT)


def _dot(a, b):
    return jnp.dot(a, b, preferred_element_type=F32)


def _dot_nt(a, b):
    return lax.dot_general(a, b, (((1,), (1,)), ((), ())), preferred_element_type=F32)


def _split2(x):
    hi = x.astype(BF16)
    lo = (x - hi.astype(F32)).astype(BF16)
    return hi, lo


def _dot_x2(x, w_bf16):
    hi, lo = _split2(x)
    return _dot(hi, w_bf16) + _dot(lo, w_bf16)


def _lane_block_dot_x2(x, w_bf16):
    hi, lo = _split2(x)
    n = w_bf16.shape[0]
    return jnp.concatenate([_dot(hi[:, j:j + n], w_bf16) + _dot(lo[:, j:j + n], w_bf16)
                            for j in range(0, x.shape[1], n)], axis=1)


def _dot_x3(x, w):
    xh, xl = _split2(x)
    wh, wl = _split2(w)
    return _dot(xh, wh) + (_dot(xh, wl) + _dot(xl, wh))


def _sigmoid(x):
    return 0.5 * jnp.tanh(0.5 * x) + 0.5


def _silu(x):
    return x * _sigmoid(x)


def _modulate(h, g, shift, scale):
    y = h * lax.rsqrt(jnp.mean(h * h, axis=-1, keepdims=True) + NORM_EPS)
    return y * g * (1.0 + scale) + shift


def _ada_kernel(c_ref, w_ref, b_ref, o_ref):
    o_ref[0] = _dot_x3(_silu(c_ref[...]), w_ref[0]) + b_ref[0]


def _ada_table(cc, ada_w, ada_b):
    depth, d, nd = ada_w.shape
    tn = 1024
    return pl.pallas_call(
        _ada_kernel,
        grid=(depth, nd // tn),
        in_specs=[pl.BlockSpec((MOD_ROWS, d), lambda i, j: (0, 0)),
                  pl.BlockSpec((1, d, tn), lambda i, j: (i, 0, j)),
                  pl.BlockSpec((1, 1, tn), lambda i, j: (i, 0, j))],
        out_specs=pl.BlockSpec((1, MOD_ROWS, tn), lambda i, j: (i, 0, j)),
        out_shape=jax.ShapeDtypeStruct((depth, MOD_ROWS, nd), F32),
        compiler_params=_cparams(("parallel", "parallel")),
        name="ada_table",
    )(cc, ada_w, ada_b.reshape(depth, 1, nd))


class _Tokens:
    def __init__(self, batch, n_lat, n_ctx, d):
        self.batch, self.n_lat, self.n_ctx, self.d = batch, n_lat, n_ctx, d
        self.t_lat = batch * n_lat
        self.t_all = batch * (n_lat + n_ctx)

    def rows(self, with_ctx):
        return self.t_all if with_ctx else self.t_lat

    def mod_index(self, tm):
        lat_tiles, per_batch, ctx_row = self.t_lat // tm, self.n_lat // tm, self.batch
        return lambda t, *_: (jnp.where(t < lat_tiles, t // per_batch, ctx_row), 0, 0)


def _mod_slices(m_ref, off, d):
    return [m_ref[0, :, (off + k) * d:(off + k + 1) * d] for k in range(3)]


def _resident(shape):
    return pl.BlockSpec(shape, lambda *_: (0,) * len(shape), pipeline_mode=pl.Buffered(1))


def _ffn_kernel(*refs, off, final, lat_tiles, alt):
    alt_ref = refs[0] if alt else None
    h_ref, m_ref, g_ref, wgu_ref, wd_ref, fg_ref, o_ref = refs[-7:]
    d, ff = h_ref.shape[1], wd_ref.shape[0]
    shift, scale, gate = _mod_slices(m_ref, off, d)
    h = h_ref[...]
    is_lat = pl.program_id(0) < lat_tiles
    if alt == "regroup":
        h = jnp.where(is_lat, jnp.concatenate([alt_ref[:, c, :] for c in range(alt_ref.shape[1])], axis=0), h)
    elif alt == "ctx":
        h = jnp.where(is_lat, h, alt_ref[...])
    half = h.shape[0] // 2
    for part in (slice(0, half), slice(half, 2 * half)):
        hp = h[part]
        y = _modulate(hp, g_ref[...], shift, scale).astype(BF16)
        gu = _dot(y, wgu_ref[...])
        act = (_silu(gu[:, :ff]) * gu[:, ff:]).astype(BF16)
        out = hp + (0.5 * gate) * _dot(act, wd_ref[...])
        if final:
            out = out * lax.rsqrt(jnp.mean(out * out, axis=-1, keepdims=True) + NORM_EPS) * fg_ref[...]
        o_ref[part, :] = out


def _half_ffn(tok, h, mods, g, w_gu, w_down, final_g, *, off, with_ctx, final=False, regroup=None, ctx_rows=None):
    d = tok.d
    tm = 512
    rows = tok.rows(with_ctx)
    lat_tiles = tok.t_lat // tm
    lat_only = lambda t: (jnp.minimum(t, lat_tiles - 1), 0)
    ctx_only = lambda t: (jnp.maximum(t - lat_tiles, 0), 0)
    ins, specs, alt = [], [], None
    h_spec = pl.BlockSpec((tm, d), lambda t: (t, 0))
    if regroup is not None:
        alt = "regroup"
        groups = tm // (tok.n_lat // regroup)
        per_batch = regroup // groups
        ins.append(h.reshape(h.shape[0] // regroup, regroup, d))
        specs.append(pl.BlockSpec((tm // groups, groups, d),
                                  lambda t: (lat_only(t)[0] // per_batch, lat_only(t)[0] % per_batch, 0)))
        last = h.shape[0] // tm - 1
        h_spec = pl.BlockSpec((tm, d), lambda t: (jnp.clip(t, min(lat_tiles, last), last), 0))
    elif ctx_rows is not None:
        alt = "ctx"
        ins.append(ctx_rows)
        specs.append(pl.BlockSpec((tm, d), ctx_only))
        h_spec = pl.BlockSpec((tm, d), lat_only)
    return pl.pallas_call(
        functools.partial(_ffn_kernel, off=off, final=final, lat_tiles=lat_tiles, alt=alt),
        grid=(rows // tm,),
        in_specs=specs + [h_spec,
                          pl.BlockSpec((1, 1, N_MOD * d), tok.mod_index(tm)),
                          pl.BlockSpec((1, d), lambda t: (0, 0)),
                          _resident(w_gu.shape),
                          _resident(w_down.shape),
                          pl.BlockSpec((1, d), lambda t: (0, 0))],
        out_specs=pl.BlockSpec((tm, d), lambda t: (t, 0)),
        out_shape=jax.ShapeDtypeStruct((rows, d), F32),
        compiler_params=_cparams(("parallel",)),
        name="half_ffn",
    )(*ins, h, mods, g.reshape(1, d), w_gu, w_down, final_g.reshape(1, d))


def _mix_in_kernel(h_ref, m_ref, g_ref, *refs):
    d = h_ref.shape[1]
    w_refs, z_refs = refs[:len(refs) // 2], refs[len(refs) // 2:]
    shift, scale, _ = _mod_slices(m_ref, 3, d)
    y = _modulate(h_ref[...], g_ref[...], shift, scale).astype(BF16)
    for w_ref, z_ref in zip(w_refs, z_refs):
        z_ref[...] = _dot(y, w_ref[...]).astype(z_ref.dtype)


def _mix_in(tok, h, mods, g, weights, out_dtypes, *, with_ctx):
    d = tok.d
    tm = 512
    rows = tok.rows(with_ctx)
    return pl.pallas_call(
        _mix_in_kernel,
        grid=(rows // tm,),
        in_specs=[pl.BlockSpec((tm, d), lambda t: (t, 0)),
                  pl.BlockSpec((1, 1, N_MOD * d), tok.mod_index(tm)),
                  pl.BlockSpec((1, d), lambda t: (0, 0))]
                 + [_resident(w.shape) for w in weights],
        out_specs=[pl.BlockSpec((tm, w.shape[1]), lambda t: (t, 0)) for w in weights],
        out_shape=[jax.ShapeDtypeStruct((rows, w.shape[1]), dt) for w, dt in zip(weights, out_dtypes)],
        compiler_params=_cparams(("parallel",)),
        name="mix_in",
    )(h, mods, g.reshape(1, d), *weights)


def _feat_kernel(z_ref, zp_ref, zn_ref, mu_ref, wup_ref, aup_ref, gup_ref, w0_ref, a0_ref, kk_ref, ka_ref, rk_ref,
                 ones_ref, r_out, v_out, kap_out, lw_out, kd_out, b_out, g_out, bonus_out, *, lat_tiles, d_rwkv):
    i = pl.program_id(1)
    tt = z_ref.shape[0]
    is_ctx = i == lat_tiles
    z = z_ref[...].astype(F32)
    first_of_seq = jnp.logical_or(i == 0, is_ctx)
    last_of_seq = jnp.logical_or(i == lat_tiles - 1, is_ctx)
    halo = zp_ref.shape[0]
    prev_row = jnp.where(first_of_seq, 0.0, zp_ref[halo - 1:halo, :].astype(F32))
    next_row = jnp.where(last_of_seq, 0.0, zn_ref[0:1, :].astype(F32))
    row = lax.broadcasted_iota(jnp.int32, (tt, 1), 0)
    z_prev = jnp.where(row == 0, prev_row, pltpu.roll(z, 1, axis=0))
    z_next = jnp.where(row == tt - 1, next_row, pltpu.roll(z, tt - 1, axis=0))
    z = z + mu_ref[...] * (0.5 * (z_prev + z_next) - z)

    dr = d_rwkv
    r, k, v = z[:, :dr], z[:, dr:2 * dr], z[:, 2 * dr:3 * dr]
    low = z[:, 3 * dr:3 * dr + DECAY_RANK + ICLR_RANK]
    gd = z[:, 3 * dr + DECAY_RANK + ICLR_RANK:]
    ones = ones_ref[...]

    kap = k * kk_ref[...]
    kap = kap * lax.rsqrt(jnp.maximum(_lane_block_dot_x2(kap * kap, ones), KAP_NORM_EPS * KAP_NORM_EPS))
    tanh_low = jnp.tanh(low)
    low_bf16 = low.astype(BF16)
    kd_sum = jnp.zeros_like(k)
    for dd in range(2):
        lw_out[dd, 0] = -DECAY_SCALE * _sigmoid(w0_ref[dd] + _dot_x2(tanh_low, wup_ref[dd]))
        a = _sigmoid(a0_ref[dd] + _dot(low_bf16, aup_ref[dd]))
        kd = k * (1.0 + (a - 1.0) * ka_ref[...])
        kd_out[dd, 0] = kd.astype(kd_out.dtype)
        b_out[dd, 0] = (kap * a).astype(b_out.dtype)
        kd_sum = kd_sum + kd
    r_out[0] = r.astype(r_out.dtype)
    v_out[0] = v.astype(v_out.dtype)
    kap_out[0] = kap.astype(kap_out.dtype)
    g_out[0] = _dot(_sigmoid(gd).astype(BF16), gup_ref[...]).astype(g_out.dtype)
    bonus_out[0] = (_lane_block_dot_x2(r * kd_sum * rk_ref[...], ones) * v).astype(bonus_out.dtype)


def _rwkv_features(tok, za, p):
    batch, n, nc = tok.batch, tok.n_lat, tok.n_ctx
    t = n + nc
    da = za.shape[1]
    dr = p["k_k"].shape[-1]
    tt = FEAT_TILE
    lat_tiles = n // tt
    ctx_block = tok.t_lat // tt
    halo = 16
    hb = tt // halo
    n_hblk = za.shape[0] // halo
    row = lambda name: p[name].reshape(1, dr)
    tile = lambda b, i: jnp.where(i < lat_tiles, b * lat_tiles + i, ctx_block + b)
    seq_spec = pl.BlockSpec((1, tt, dr), lambda b, i: (b, i, 0))
    dir_spec = pl.BlockSpec((2, 1, tt, dr), lambda b, i: (0, b, i, 0))
    full = lambda a: pl.BlockSpec(a.shape, lambda b, i: (0,) * a.ndim)
    consts = [p["mu"].reshape(1, da), p["w_up_pad"], p["a_up_pad"], p["g_up"],
              p["w0"].reshape(2, 1, dr), p["a0"].reshape(2, 1, dr), row("k_k"), row("k_a"), row("r_k"),
              p["head_ones"]]
    seq = lambda dt: jax.ShapeDtypeStruct((batch, t, dr), dt)
    dirs = lambda dt: jax.ShapeDtypeStruct((2, batch, t, dr), dt)
    return pl.pallas_call(
        functools.partial(_feat_kernel, lat_tiles=lat_tiles, d_rwkv=dr),
        grid=(batch, t // tt),
        in_specs=[pl.BlockSpec((tt, da), lambda b, i: (tile(b, i), 0)),
                  pl.BlockSpec((halo, da), lambda b, i: (jnp.maximum(tile(b, i) * hb - 1, 0), 0)),
                  pl.BlockSpec((halo, da), lambda b, i: (jnp.minimum((tile(b, i) + 1) * hb, n_hblk - 1), 0))]
                 + [full(a) for a in consts],
        out_specs=[seq_spec, seq_spec, seq_spec, dir_spec, dir_spec, dir_spec, seq_spec, seq_spec],
        out_shape=[seq(BF16), seq(BF16), seq(BF16), dirs(F32), dirs(BF16), dirs(BF16), seq(BF16), seq(BF16)],
        compiler_params=_cparams(("parallel", "parallel")),
        name="rwkv_features",
    )(za, za, za, *consts)


def _chunk_masks(ch, rev):
    ri = lax.broadcasted_iota(jnp.int32, (PAIR, PAIR), 0)
    ci = lax.broadcasted_iota(jnp.int32, (PAIR, PAIR), 1)
    same_head = (ri // ch) == (ci // ch)
    rt, ct = ri % ch, ci % ch
    before, upto = (ct > rt, ct >= rt) if rev else (ct < rt, ct <= rt)
    return dict(same_head=same_head, diag=ri == ci,
                strict=jnp.logical_and(same_head, before), incl=jnp.logical_and(same_head, upto))


def _scan_kernel(rf_ref, vf_ref, kapf_ref, lwf_ref, kdf_ref, bf_ref, rb_ref, vb_ref, kapb_ref, lwb_ref, kdb_ref,
                 bb_ref, yf_ref, yb_ref, s_scr):
    c = pl.program_id(1)
    ch = CHUNK
    n_sub = rf_ref.shape[1] // ch
    n_pairs = rf_ref.shape[2] // PAIR

    @pl.when(c == 0)
    def _():
        s_scr[...] = jnp.zeros_like(s_scr)

    row_id = lax.broadcasted_iota(jnp.int32, (ch, 1), 0)
    lane_head = lax.broadcasted_iota(jnp.int32, (ch, PAIR), 1) // HEAD
    head0, head1 = lane_head == 0, lane_head == 1

    def stack_heads(x):
        return jnp.concatenate([jnp.where(head0, x, 0.0), jnp.where(head1, x, 0.0)], axis=0)

    def fold_heads(x):
        return x[:ch] + x[ch:]

    chains = []
    dirs = ((rf_ref, vf_ref, kapf_ref, lwf_ref, kdf_ref, bf_ref, yf_ref),
            (rb_ref, vb_ref, kapb_ref, lwb_ref, kdb_ref, bb_ref, yb_ref))
    for d, (r_ref, v_ref, kap_ref, lw_ref, kd_ref, b_ref, y_ref) in enumerate(dirs):
        mk = _chunk_masks(ch, rev=(d == 1))
        for sub in (range(n_sub) if d == 0 else reversed(range(n_sub))):
            rows = slice(sub * ch, (sub + 1) * ch)
            lw_all = lw_ref[0, 0, rows, :]
            cl_all = lw_all
            for step in (1 << k for k in range(int(np.log2(ch)))):
                if d == 0:
                    cl_all = cl_all + jnp.where(row_id >= step, pltpu.roll(cl_all, step, axis=0), 0.0)
                else:
                    cl_all = cl_all + jnp.where(row_id < ch - step, pltpu.roll(cl_all, ch - step, axis=0), 0.0)
            for p in range(n_pairs):
                sl = slice(p * PAIR, (p + 1) * PAIR)
                f32 = lambda ref, *lead: ref[(*lead, rows, sl)].astype(F32)
                chains.append(dict(d=d, p=p, sl=sl, rows=rows, mk=mk, y_ref=y_ref, lw=lw_all[:, sl], cl=cl_all[:, sl],
                                   r=f32(r_ref, 0), v=f32(v_ref, 0), kap=f32(kap_ref, 0),
                                   kd=f32(kd_ref, 0, 0), b=f32(b_ref, 0, 0)))

    for q in chains:
        lw, cl, mk = q["lw"], q["cl"], q["mk"]
        tot = jnp.sum(lw, axis=0, keepdims=True)
        g_inv, g_rem = jnp.exp(-cl), jnp.exp(tot - cl)
        kq = stack_heads(q["kap"] * jnp.exp(cl - lw))
        rq = stack_heads(q["r"] * jnp.exp(cl))
        kdd, bdd = q["kd"] * g_inv, q["b"] * g_inv
        gram = _dot_nt(jnp.concatenate([kq, rq], axis=0).astype(BF16),
                       jnp.concatenate([bdd, bdd, kdd, kdd], axis=0).astype(BF16))
        q.update(kq=kq, rq=rq, g_tot=jnp.exp(tot),
                 kb=jnp.concatenate([q["kd"] * g_rem, q["b"] * g_rem], axis=0).astype(BF16),
                 l_b=jnp.where(mk["strict"], gram[:PAIR, :PAIR], 0.0),
                 l_k=jnp.where(mk["strict"], gram[:PAIR, PAIR:], 0.0),
                 pkb=jnp.concatenate([jnp.where(mk["incl"], gram[PAIR:, PAIR:], 0.0),
                                      jnp.where(mk["incl"], -gram[PAIR:, :PAIR], 0.0)], axis=1).astype(BF16))
    for q in chains:
        q["v2"] = jnp.concatenate([q["v"], q["v"]], axis=0).astype(BF16)
        lkv = jnp.where(q["mk"]["same_head"], _dot(q["l_k"].astype(BF16), q["v2"]), 0.0)
        q["rhs"] = jnp.concatenate([q["kq"], lkv], axis=1).astype(BF16)
        q["lb"] = q["l_b"].astype(BF16)
    for q in chains:
        q["lp"] = _dot(q["lb"], q["lb"]).astype(BF16)
        q["t"] = jnp.where(q["mk"]["diag"], 1.0, 0.0) - q["l_b"]
    n_sq = int(np.log2(ch))
    for k in range(1, n_sq):
        for q in chains:
            w = jnp.concatenate([q["t"].astype(BF16), q["lp"]], axis=1) if k < n_sq - 1 else q["t"].astype(BF16)
            q["both"] = _dot(q["lp"], w)
        for q in chains:
            q["t"] = q["t"] + q["both"][:, :PAIR]
            if k < n_sq - 1:
                q["lp"] = q["both"][:, PAIR:].astype(BF16)
    for q in chains:
        q["x"] = _dot(q["t"].astype(BF16), q["rhs"])
    for q in chains:
        x = q["x"]
        mk = q["mk"]
        xk, w1 = x[:, :PAIR], x[:, PAIR:]
        rhs = jnp.concatenate([jnp.concatenate([q["v2"], jnp.zeros_like(q["v2"])], axis=1),
                               jnp.concatenate([w1, xk], axis=1).astype(BF16)], axis=0)
        big = _dot(q["pkb"], rhs)
        q["y0"] = fold_heads(jnp.where(mk["same_head"], big[:, :PAIR], 0.0))
        q["r_eff"] = (q["rq"] + big[:, PAIR:]).astype(BF16)
        lhs = jnp.concatenate([jnp.concatenate([q["v"], -fold_heads(w1)], axis=0),
                               jnp.concatenate([jnp.zeros_like(q["v"]), -fold_heads(xk)], axis=0)], axis=1)
        upd = _dot(lhs.T.astype(BF16), q["kb"])
        q["z0"] = jnp.where(mk["same_head"], upd[:PAIR], 0.0)
        q["a"] = (jnp.where(mk["diag"], q["g_tot"], 0.0) + jnp.where(mk["same_head"], upd[PAIR:], 0.0)).astype(BF16)
    for q in chains:
        s0 = s_scr[q["d"], q["p"]].astype(BF16)
        q["y_ref"][0, q["rows"], q["sl"]] = (q["y0"] + fold_heads(_dot_nt(q["r_eff"], s0))).astype(BF16)
        s_scr[q["d"], q["p"]] = _dot(s0, q["a"]) + q["z0"]


def _rwkv_scan(r, v, kap, lw, kd, b, n_ctx):
    batch, t, dr = r.shape
    rows = SCAN_CHUNKS * CHUNK
    assert t % rows == 0 and n_ctx % rows == 0
    n_blocks, ctx_blocks = t // rows, n_ctx // rows
    lat_blocks = n_blocks - ctx_blocks

    def fwd_block(c):
        return jnp.where(c < ctx_blocks, lat_blocks + c, c - ctx_blocks)

    def bwd_block(c):
        return n_blocks - 1 - c

    fwd_seq = pl.BlockSpec((1, rows, dr), lambda bb, c: (bb, fwd_block(c), 0))
    bwd_seq = pl.BlockSpec((1, rows, dr), lambda bb, c: (bb, bwd_block(c), 0))
    fwd_dir = pl.BlockSpec((1, 1, rows, dr), lambda bb, c: (0, bb, fwd_block(c), 0))
    bwd_dir = pl.BlockSpec((1, 1, rows, dr), lambda bb, c: (1, bb, bwd_block(c), 0))
    y_shape = jax.ShapeDtypeStruct((batch, t, dr), BF16)
    return pl.pallas_call(
        _scan_kernel,
        grid=(batch, n_blocks),
        in_specs=[fwd_seq, fwd_seq, fwd_seq, fwd_dir, fwd_dir, fwd_dir,
                  bwd_seq, bwd_seq, bwd_seq, bwd_dir, bwd_dir, bwd_dir],
        out_specs=[fwd_seq, bwd_seq],
        out_shape=[y_shape, y_shape],
        scratch_shapes=[pltpu.VMEM((2, dr // PAIR, PAIR, PAIR), F32)],
        compiler_params=_cparams(("parallel", "arbitrary")),
        name="rwkv_scan",
    )(r, v, kap, lw, kd, b, r, v, kap, lw, kd, b)


def _even_out_kernel(h_ref, m_ref, yf_ref, yb_ref, g_ref, bonus_ref, fl_ref, fc_ref, gnw_ref, gnb_ref, mean_ref,
                     wo_ref, wf_ref, o_ref, *, lat_tiles):
    d = h_ref.shape[1]
    y = yf_ref[0].astype(F32) + yb_ref[0].astype(F32)
    mean_mat = mean_ref[...]
    dev = y - _lane_block_dot_x2(y, mean_mat)
    var = _lane_block_dot_x2(dev * dev, mean_mat)
    yn = dev * lax.rsqrt(var + GN_EPS) * gnw_ref[...] + gnb_ref[...]
    o = ((yn + bonus_ref[0].astype(F32)) * g_ref[0].astype(F32)).astype(BF16)
    f = jnp.where(pl.program_id(0) < lat_tiles, fl_ref[...], fc_ref[...]).astype(BF16)
    mixed = _dot(o, wo_ref[...]) + _dot(f, wf_ref[...])
    o_ref[...] = h_ref[...] + m_ref[0, :, 5 * d:6 * d] * mixed


def _even_out(tok, h, mods, seq_arrays, f_lat, f_ctx, p, *, with_ctx):
    d, batch, n = tok.d, tok.batch, tok.n_lat
    dr = p["gn_w"].shape[-1]
    tt = FEAT_TILE
    per_batch = n // tt
    lat_tiles = batch * per_batch
    seq_spec = pl.BlockSpec((1, tt, dr), lambda t: (jnp.where(t < lat_tiles, t // per_batch, t - lat_tiles),
                                                    jnp.where(t < lat_tiles, t % per_batch, per_batch), 0))
    df = f_lat.shape[-1]
    vec = lambda a: pl.BlockSpec((1, a.shape[-1]), lambda t: (0, 0))
    return pl.pallas_call(
        functools.partial(_even_out_kernel, lat_tiles=lat_tiles),
        grid=(tok.rows(with_ctx) // tt,),
        in_specs=[pl.BlockSpec((tt, d), lambda t: (t, 0)),
                  pl.BlockSpec((1, 1, N_MOD * d), tok.mod_index(tt))]
                 + [seq_spec] * len(seq_arrays)
                 + [pl.BlockSpec((tt, df), lambda t: (jnp.minimum(t, lat_tiles - 1), 0)),
                    pl.BlockSpec((tt, df), lambda t: (jnp.maximum(t - lat_tiles, 0), 0)),
                    vec(p["gn_w"]), vec(p["gn_b"]), _resident(p["head_mean"].shape),
                    _resident(p["w_out_o"].shape), _resident(p["w_out_f"].shape)],
        out_specs=pl.BlockSpec((tt, d), lambda t: (t, 0)),
        out_shape=jax.ShapeDtypeStruct((tok.rows(with_ctx), d), F32),
        compiler_params=_cparams(("parallel",)),
        name="even_out",
    )(h, mods, *seq_arrays, f_lat.reshape(-1, df), f_ctx.reshape(-1, df), p["gn_w"].reshape(1, dr),
      p["gn_b"].reshape(1, dr), p["head_mean"], p["w_out_o"], p["w_out_f"])


def _fourier_kernel(cs_ref, u_ref, cg_ref, sg_ref, o_ref, stacked_scr):
    n, ch = u_ref.shape[0] // FOURIER_BATCH, u_ref.shape[1]

    @pl.when(pl.program_id(1) == 0)
    def _():
        for k in range(FOURIER_BATCH):
            u_hi, u_lo = _split2(u_ref[k * n:(k + 1) * n, :])
            cols = slice(k * ch, (k + 1) * ch)
            stacked_scr[:n, cols] = (_dot(u_hi, cg_ref[...]) + _dot(u_lo, cg_ref[...])).astype(BF16)
            stacked_scr[n:, cols] = (_dot(u_hi, sg_ref[...]) + _dot(u_lo, sg_ref[...])).astype(BF16)

    out = _dot(cs_ref[...], stacked_scr[...])
    for k in range(FOURIER_BATCH):
        o_ref[k] = out[:, k * ch:(k + 1) * ch]


def _dft_tables(n, groups, col_major):
    n_hi = n // GRID_W
    idx = jnp.arange(n, dtype=jnp.int32)
    pos = (idx % n_hi) * GRID_W + idx // n_hi if col_major else idx
    k = pos[:, None]
    ang_hi = ((k * (jnp.arange(n_hi, dtype=jnp.int32) * GRID_W)[None, :]) % n).astype(F32) * (2.0 * np.pi / n)
    ang_lo = ((k * jnp.arange(GRID_W, dtype=jnp.int32)[None, :]) % n).astype(F32) * (2.0 * np.pi / n)
    scale = 1.0 / np.sqrt(n * FOURIER_GROUP)
    c_hi, s_hi = jnp.cos(ang_hi) * scale, jnp.sin(ang_hi) * scale
    c_lo, s_lo = jnp.cos(ang_lo), jnp.sin(ang_lo)
    if col_major:
        hi, lo = (lambda a: a[:, None, :]), (lambda a: a[:, :, None])
    else:
        hi, lo = (lambda a: a[:, :, None]), (lambda a: a[:, None, :])
    cos = (hi(c_hi) * lo(c_lo) - hi(s_hi) * lo(s_lo)).reshape(n, n)
    sin = (hi(s_hi) * lo(c_lo) + hi(c_hi) * lo(s_lo)).reshape(n, n)
    cs = jnp.concatenate([cos, -sin], axis=1).astype(BF16)
    kg = np.arange(FOURIER_GROUP)
    ang_g = 2.0 * np.pi * ((kg[:, None] * kg[None, :]) % FOURIER_GROUP) / FOURIER_GROUP
    eye = np.eye(groups)
    cg = jnp.asarray(np.kron(eye, np.cos(ang_g)), F32).astype(BF16)
    sg = jnp.asarray(np.kron(eye, np.sin(ang_g)), F32).astype(BF16)
    return cs, cg, sg


def _fourier_mix(zf, batch, n, first_block, col_major=False):
    ch = zf.shape[1]
    cs, cg, sg = _dft_tables(n, ch // FOURIER_GROUP, col_major)
    tn = min(n, 512)
    nb = FOURIER_BATCH
    assert batch % nb == 0 and first_block % nb == 0
    return pl.pallas_call(
        _fourier_kernel,
        grid=(batch // nb, n // tn),
        in_specs=[pl.BlockSpec((tn, 2 * n), lambda b, i: (i, 0)),
                  pl.BlockSpec((nb * n, ch), lambda b, i: (first_block // nb + b, 0)),
                  pl.BlockSpec((ch, ch), lambda b, i: (0, 0)),
                  pl.BlockSpec((ch, ch), lambda b, i: (0, 0))],
        out_specs=pl.BlockSpec((nb, tn, ch), lambda b, i: (b, i, 0)),
        out_shape=jax.ShapeDtypeStruct((batch, n, ch), F32),
        scratch_shapes=[pltpu.VMEM((2 * n, nb * ch), BF16)],
        compiler_params=_cparams(("parallel", "arbitrary")),
        name="fourier_mix",
    )(cs, zf, cg, sg)


def _odd_kernel(h_ref, m_ref, z_ref, zp_ref, zn_ref, cw_ref, cb_ref, cg_ref, pw_ref, ps_ref, wu_ref, wp_ref,
                o_ref, u_scr, *, lat_tiles, lat_seq_tiles, n_lat, n_ctx, d_conv):
    i = pl.program_id(0)
    tt, d = h_ref.shape
    is_lat = i < lat_tiles
    seq_tile = jnp.where(is_lat, i % lat_seq_tiles, 0)
    seq_tiles = jnp.where(is_lat, lat_seq_tiles, n_ctx // tt)
    seq_len = jnp.where(is_lat, n_lat, n_ctx)
    zp = jnp.where(seq_tile == 0, 0.0, zp_ref[...])
    zn = jnp.where(seq_tile == seq_tiles - 1, 0.0, zn_ref[...])
    ext = jnp.concatenate([zp, z_ref[...], zn], axis=0)

    dc = d_conv
    rows = tt + 2 * HALO
    glu = ext[:, :dc] * _sigmoid(ext[:, dc:2 * dc])
    u_scr[0] = glu
    for r in range(1, 8):
        u_scr[r] = pltpu.roll(glu, rows - r, axis=0)
    acc = jnp.zeros((tt, dc), F32) + cb_ref[...]
    for k in range(CONV_WIDTH):
        shift = HALO - CONV_WIDTH // 2 + k
        acc = acc + cw_ref[k:k + 1, :] * u_scr[shift % 8, pl.ds(shift - shift % 8, tt), :]
    u = acc * lax.rsqrt(jnp.mean(acc * acc, axis=-1, keepdims=True) + NORM_EPS) * cg_ref[...]
    u = _silu(u)

    q = ext[:, 2 * dc:]
    sums, win, step = [], q, 1
    for width in POOL_WIDTHS:
        if width == 2:
            win = q + pltpu.roll(q, 1, axis=0)
        else:
            win = pltpu.roll(win, step, axis=0) + pltpu.roll(win, rows - step, axis=0)
            step *= 2
        sums.append(win[HALO:HALO + tt])
    n_groups = len(POOL_WIDTHS)
    group = lax.broadcasted_iota(jnp.int32, (tt, q.shape[1]), 1) // (q.shape[1] // n_groups)
    pos = seq_tile * tt + lax.broadcasted_iota(jnp.int32, (tt, q.shape[1]), 0)
    total = sums[-1]
    width_of = jnp.full(group.shape, POOL_WIDTHS[-1], jnp.int32)
    for gi in range(n_groups - 2, -1, -1):
        total = jnp.where(group == gi, sums[gi], total)
        width_of = jnp.where(group == gi, POOL_WIDTHS[gi], width_of)
    lo = jnp.maximum(pos - width_of // 2, 0)
    hi = jnp.minimum(pos + (width_of - 1 - width_of // 2), seq_len - 1)
    pooled = total / (hi - lo + 1).astype(F32) - q[HALO:HALO + tt]
    pooled = _dot(pooled.astype(BF16), pw_ref[...]) * ps_ref[...]

    mixed = _dot(u.astype(BF16), wu_ref[...]) + _dot(pooled.astype(BF16), wp_ref[...])
    o_ref[...] = h_ref[...] + m_ref[0, :, 5 * d:6 * d] * mixed


def _odd_mixer(tok, h, mods, z, p, *, with_ctx):
    d, d_in = tok.d, z.shape[1]
    dc = p["conv_w"].shape[1]
    dp = d_in - 2 * dc
    tt = 256
    rows = tok.rows(with_ctx)
    hb, n_hblk = tt // HALO, z.shape[0] // HALO
    full = lambda a: pl.BlockSpec(a.shape, lambda i: (0,) * a.ndim)
    consts = [p["conv_w"], p["conv_b"].reshape(1, dc), p["cnorm_g"].reshape(1, dc), p["pool_w_bd"],
              p["pool_scale"].reshape(1, dp), p["w_out_u"], p["w_out_p"]]
    return pl.pallas_call(
        functools.partial(_odd_kernel, lat_tiles=tok.t_lat // tt, lat_seq_tiles=tok.n_lat // tt,
                          n_lat=tok.n_lat, n_ctx=tok.n_ctx, d_conv=dc),
        grid=(rows // tt,),
        in_specs=[pl.BlockSpec((tt, d), lambda i: (i, 0)),
                  pl.BlockSpec((1, 1, N_MOD * d), tok.mod_index(tt)),
                  pl.BlockSpec((tt, d_in), lambda i: (i, 0)),
                  pl.BlockSpec((HALO, d_in), lambda i: (jnp.maximum(i * hb - 1, 0), 0)),
                  pl.BlockSpec((HALO, d_in), lambda i: (jnp.minimum((i + 1) * hb, n_hblk - 1), 0))]
                 + [full(a) for a in consts],
        out_specs=pl.BlockSpec((tt, d), lambda i: (i, 0)),
        out_shape=jax.ShapeDtypeStruct((rows, d), F32),
        scratch_shapes=[pltpu.VMEM((8, tt + 2 * HALO, dc), F32)],
        compiler_params=_cparams(("parallel",)),
        name="odd_mixer",
    )(h, mods, z, z, z, *consts)


def _even_mixer(tok, h, mods, norm_g, p, col_major, ctx_out):
    za, zf = _mix_in(tok, h, mods, norm_g, [p["w_in_a"], p["w_in_f"]], [BF16, F32], with_ctx=True)
    f_lat = _fourier_mix(zf, tok.batch, tok.n_lat, 0, col_major)
    f_ctx = _fourier_mix(zf, tok.batch, tok.n_ctx, tok.t_lat // tok.n_ctx) if ctx_out else f_lat
    r, v, kap, lw, kd, bb, g, bonus = _rwkv_features(tok, za, p)
    y_fwd, y_bwd = _rwkv_scan(r, v, kap, lw, kd, bb, tok.n_ctx)
    return _even_out(tok, h, mods, [y_fwd, y_bwd, g, bonus], f_lat, f_ctx, p, with_ctx=ctx_out)


def _pad_rows(w, before, total):
    return jnp.pad(w, ((0, 0), (before, total - before - w.shape[1]), (0, 0)))


def _block_diag(blocks):
    g, a, b = blocks.shape
    eye = jnp.eye(g, dtype=blocks.dtype)
    return (eye[:, None, :, None] * blocks[:, :, None, :]).reshape(g * a, g * b)


def kernel(x, c, ctx, c_ctx, ada_w, ada_b, norm_g, ffn1_w_gu, ffn1_w_down, ffn2_w_gu, ffn2_w_down,
           e_w_in, e_mu, e_w0, e_w_up, e_a0, e_a_up, e_g_up, e_k_k, e_k_a, e_r_k, e_gn_w, e_gn_b, e_w_out,
           o_w_in, o_conv_w, o_conv_b, o_cnorm_g, o_pool_w, o_pool_scale, o_w_out, final_g):
    batch, n_lat, d = x.shape
    n_ctx = ctx.shape[1]
    depth = ada_w.shape[0]
    tok = _Tokens(batch, n_lat, n_ctx, d)
    assert batch < MOD_ROWS and n_lat % 512 == 0 and (batch * n_ctx) % 512 == 0 and n_ctx == FEAT_TILE
    assert n_lat % (GRID_W * 8) == 0 and FEAT_TILE % (n_lat // GRID_W) == 0 and FEAT_TILE % GRID_W == 0

    cc = jnp.zeros((MOD_ROWS, d), F32).at[:batch].set(c).at[batch].set(c_ctx)
    mods_all = _ada_table(cc, ada_w, ada_b).reshape(depth, MOD_ROWS, 1, N_MOD * d)
    h = x.reshape(batch * n_lat, d)

    dr = e_k_k.shape[-1]
    low_rank = DECAY_RANK + ICLR_RANK
    head_ones = _block_diag(jnp.ones((HEAD_SUM_LANES // HEAD, HEAD, HEAD), F32))
    n_rows = n_lat // GRID_W
    uses_ctx = lambda k: not (k == depth - 1 and k % 2 == 1)
    for i in range(depth):
        j = i // 2
        with_ctx = uses_ctx(i)
        ctx_out = with_ctx and any(uses_ctx(k) for k in range(i + 1, depth))
        col_major = i % 2 == 0 and j % 2 == 1
        mods = mods_all[i]
        h = _half_ffn(tok, h, mods, norm_g[i, 0], ffn1_w_gu[i].astype(BF16), ffn1_w_down[i].astype(BF16), final_g,
                      off=0, with_ctx=with_ctx, regroup=GRID_W if col_major else None,
                      ctx_rows=ctx.reshape(batch * n_ctx, d) if i == 0 else None)
        if i % 2 == 0:
            da = e_mu.shape[-1]
            p = dict(w_in_a=e_w_in[j, :, :da].astype(BF16), w_in_f=e_w_in[j, :, da:].astype(BF16),
                     mu=e_mu[j], w0=e_w0[j], a0=e_a0[j], g_up=e_g_up[j].astype(BF16),
                     w_up_pad=_pad_rows(e_w_up[j], 0, low_rank).astype(BF16),
                     a_up_pad=_pad_rows(e_a_up[j], DECAY_RANK, low_rank).astype(BF16),
                     k_k=e_k_k[j], k_a=e_k_a[j], r_k=e_r_k[j], gn_w=e_gn_w[j], gn_b=e_gn_b[j],
                     w_out_o=e_w_out[j, :dr].astype(BF16), w_out_f=e_w_out[j, dr:].astype(BF16),
                     head_ones=head_ones.astype(BF16),
                     head_mean=(head_ones / HEAD).astype(BF16))
            h = _even_mixer(tok, h, mods, norm_g[i, 1], p, col_major, ctx_out)
        else:
            dc = o_conv_w.shape[-1]
            p = dict(conv_w=o_conv_w[j], conv_b=o_conv_b[j], cnorm_g=o_cnorm_g[j],
                     pool_w_bd=_block_diag(o_pool_w[j]).astype(BF16), pool_scale=o_pool_scale[j],
                     w_out_u=o_w_out[j, :dc].astype(BF16), w_out_p=o_w_out[j, dc:].astype(BF16))
            (z,) = _mix_in(tok, h, mods, norm_g[i, 1], [o_w_in[j].astype(BF16)], [F32], with_ctx=with_ctx)
            h = _odd_mixer(tok, h, mods, z, p, with_ctx=with_ctx)
        h = _half_ffn(tok, h, mods, norm_g[i, 2], ffn2_w_gu[i].astype(BF16), ffn2_w_down[i].astype(BF16), final_g,
                      off=6, with_ctx=ctx_out, final=(i == depth - 1), regroup=n_rows if col_major else None)
    return h[:tok.t_lat].reshape(batch, n_lat, d)
```
